```python
import math
import jax, jax.numpy as jnp
from jax import lax
import numpy as np

D_MODEL = 1024
BATCH = 4
SEQ = 4096
DEPTH = 2

MEM_LEN = 256
D_MIX = D_MODEL
GLA_WIDTH = 3 * D_MIX // 8
GLA_HEADS = 4
GLA_DV = GLA_WIDTH // GLA_HEADS
GLA_DK = GLA_DV // 2
GLA_RANK = 16
GLA_TAU = 16.0
GLA_CHUNK = 64
DSA_WIDTH = 3 * D_MIX // 8
DSA_DH = 64
DSA_HEADS = DSA_WIDTH // DSA_DH
IDX_HEADS = 8
IDX_DIM = 64
TOPK_MAX = 256
Q_BLOCK = 128
CONV_CH = D_MIX - GLA_WIDTH - DSA_WIDTH
CONV_WIDTH = 31
NUM_BUCKETS = 32
MAX_DISTANCE = 128
X_HEADS = 4
X_DH = 64
N_GROUPS = 4
EXPERTS_PER_GROUP = 4
N_EXPERTS = N_GROUPS * EXPERTS_PER_GROUP
TOPK_EXPERTS = 2
D_EXPERT = 512
NORM_EPS = 1e-6
IN_SIZES = (GLA_HEADS * GLA_DK, GLA_HEADS * GLA_DK, GLA_WIDTH, GLA_RANK, GLA_WIDTH,
            DSA_WIDTH, DSA_WIDTH, DSA_WIDTH, IDX_HEADS * IDX_DIM, IDX_DIM, IDX_HEADS,
            2 * CONV_CH)
D_IN = sum(IN_SIZES)

kernel_name = "hybrid_gla_dsa_conformer_hmoe"


def rms_norm(x, g):
    xf = x.astype(jnp.float32)
    y = xf * lax.rsqrt(jnp.mean(xf * xf, axis=-1, keepdims=True) + NORM_EPS)
    return (y * g.astype(jnp.float32)).astype(x.dtype)


def t5_bucket(dist):
    max_exact = NUM_BUCKETS // 2
    d = jnp.maximum(dist, 1).astype(jnp.float32)
    large = max_exact + (jnp.log(d / max_exact) / math.log(MAX_DISTANCE / max_exact)
                         * (NUM_BUCKETS - max_exact)).astype(jnp.int32)
    large = jnp.minimum(large, NUM_BUCKETS - 1)
    return jnp.where(dist < max_exact, dist, large)


def gla_chunked(q, k, v, log_a):
    B, S, H, DK = q.shape
    DV = v.shape[-1]
    n = S // GLA_CHUNK

    def to_chunks(t):
        return t.astype(jnp.float32).reshape(B, n, GLA_CHUNK, H, t.shape[-1]).transpose(1, 0, 3, 2, 4)

    qc, kc, vc, ac = to_chunks(q * (DK ** -0.5)), to_chunks(k), to_chunks(v), to_chunks(log_a)
    causal = jnp.tril(jnp.ones((GLA_CHUNK, GLA_CHUNK), dtype=bool))

    def step(state, inp):
        qi, ki, vi, ai = inp
        b = jnp.cumsum(ai, axis=-2)
        diff = b[:, :, :, None, :] - b[:, :, None, :, :]
        decay = jnp.exp(jnp.where(causal[:, :, None], diff, -jnp.inf))
        attn = jnp.einsum('bhtd,bhsd,bhtsd->bhts', qi, ki, decay)
        o = (jnp.einsum('bhts,bhsv->bhtv', attn, vi)
             + jnp.einsum('bhtd,bhdv->bhtv', qi * jnp.exp(b), state))
        b_last = b[:, :, -1:, :]
        state = (jnp.exp(b_last[:, :, 0, :])[..., None] * state
                 + jnp.einsum('bhsd,bhsv->bhdv', ki * jnp.exp(b_last - b), vi))
        return state, o

    state0 = jnp.zeros((B, H, DK, DV), jnp.float32)
    _, o = lax.scan(step, state0, (qc, kc, vc, ac))
    return o.transpose(1, 0, 3, 2, 4).reshape(B, S, H, DV)


def dsa_attention(q, k, v, q_idx, k_idx, w_idx, rel_bias, topk):
    B, S, H, DH = q.shape
    n_blk = S // Q_BLOCK
    kf = k.reshape(B, S, H * DH)
    vf = v.reshape(B, S, H * DH)
    key_pos = jnp.arange(S)
    k_idx_f = k_idx.astype(jnp.float32)

    def block(i):
        start = i * Q_BLOCK
        qi = lax.dynamic_slice_in_dim(q, start, Q_BLOCK, axis=1)
        qx = lax.dynamic_slice_in_dim(q_idx, start, Q_BLOCK, axis=1).astype(jnp.float32)
        wx = lax.dynamic_slice_in_dim(w_idx, start, Q_BLOCK, axis=1).astype(jnp.float32)
        tpos = start + jnp.arange(Q_BLOCK)
        dots = jnp.einsum('bthd,bsd->bths', qx, k_idx_f) * (IDX_DIM ** -0.5)
        score = jnp.einsum('bths,bth->bts', jax.nn.relu(dots), wx)
        score = jnp.where(key_pos[None, :] <= tpos[:, None], score, -jnp.inf)
        _, sel = lax.top_k(score, topk)
        valid = sel <= tpos[None, :, None]
        ksel = jax.vmap(lambda kb, ib: kb[ib])(kf, sel).reshape(B, Q_BLOCK, topk, H, DH)
        vsel = jax.vmap(lambda vb, ib: vb[ib])(vf, sel).reshape(B, Q_BLOCK, topk, H, DH)
        logits = jnp.einsum('bthd,btkhd->bhtk', qi, ksel).astype(jnp.float32) * (DH ** -0.5)
        bucket = t5_bucket(jnp.maximum(tpos[None, :, None] - sel, 0))
        logits = logits + rel_bias[bucket].astype(jnp.float32).transpose(0, 3, 1, 2)
        logits = jnp.where(valid[:, None], logits, -jnp.inf)
        p = jax.nn.softmax(logits, axis=-1)
        return jnp.einsum('bhtk,btkhd->bthd', p.astype(vsel.dtype), vsel)

    out = lax.map(block, jnp.arange(n_blk))
    return out.transpose(1, 0, 2, 3, 4).reshape(B, S, H, DH)


def conformer_conv(u, conv_w, conv_b, ln_g, ln_b):
    a, g = jnp.split(u, 2, axis=-1)
    h = a * jax.nn.sigmoid(g)
    h = lax.conv_general_dilated(h, conv_w[:, None, :].astype(h.dtype), window_strides=(1,),
                                 padding=[(CONV_WIDTH - 1, 0)],
                                 dimension_numbers=('NWC', 'WIO', 'NWC'),
                                 feature_group_count=CONV_CH) + conv_b
    hf = h.astype(jnp.float32)
    mu = jnp.mean(hf, axis=-1, keepdims=True)
    var = jnp.mean(jnp.square(hf - mu), axis=-1, keepdims=True)
    y = (hf - mu) * lax.rsqrt(var + NORM_EPS) * ln_g.astype(jnp.float32) + ln_b.astype(jnp.float32)
    return jax.nn.silu(y).astype(u.dtype)


def memory_cross_attention(h, m, w_cq, w_ckv, qn, kn, w_co):
    B, S, _ = h.shape
    M = m.shape[1]
    q = rms_norm((h @ w_cq).reshape(B, S, X_HEADS, X_DH), qn)
    kv = (m @ w_ckv).reshape(B, M, 2, X_HEADS, X_DH)
    k = rms_norm(kv[:, :, 0], kn)
    v = kv[:, :, 1]
    logits = jnp.einsum('bshd,bmhd->bhsm', q, k).astype(jnp.float32) * (X_DH ** -0.5)
    p = jax.nn.softmax(logits, axis=-1)
    o = jnp.einsum('bhsm,bmhd->bshd', p.astype(v.dtype), v).reshape(B, S, X_HEADS * X_DH)
    return o @ w_co


def hierarchical_moe(h, w_rg, b_rg, w_re, b_re, w_gate, w_up, w_down):
    B, S, D = h.shape
    t = h.reshape(B * S, D)
    g_logits = (t @ w_rg + b_rg).astype(jnp.float32)
    g_prob = jax.nn.softmax(g_logits, axis=-1)
    g_sel = jnp.argmax(g_logits, axis=-1)
    g_w = jnp.take_along_axis(g_prob, g_sel[:, None], axis=1)[:, 0]
    e_logits = (t @ w_re + b_re).astype(jnp.float32).reshape(-1, N_GROUPS, EXPERTS_PER_GROUP)
    e_logits = jnp.take_along_axis(e_logits, g_sel[:, None, None], axis=1)[:, 0]
    top_w, top_i = lax.top_k(jax.nn.softmax(e_logits, axis=-1), TOPK_EXPERTS)
    top_w = top_w / jnp.sum(top_w, axis=-1, keepdims=True)
    expert_id = g_sel[:, None] * EXPERTS_PER_GROUP + top_i
    gates = jnp.einsum('tk,tke->te', top_w, jax.nn.one_hot(expert_id, N_EXPERTS, dtype=jnp.float32))
    gates = (gates * g_w[:, None]).astype(t.dtype)
    y = jnp.zeros_like(t)
    for e in range(N_EXPERTS):
        he = jax.nn.silu(t @ w_gate[e]) * (t @ w_up[e])
        y = y + gates[:, e:e + 1] * (he @ w_down[e])
    return y.reshape(B, S, D)


def setup_inputs(seed: int = 0) -> dict:
    key = jax.random.key(seed)
    keys = jax.random.split(key, 32)
    counter = [0]

    def nrm(shape, scale):
        k = keys[counter[0]]
        counter[0] += 1
        return scale * jax.random.normal(k, shape, jnp.float32)

    L = DEPTH
    return {
        "x": nrm((BATCH, SEQ, D_MODEL), 1.0),
        "mem": nrm((BATCH, MEM_LEN, D_MODEL), 1.0),
        "rel_bias": nrm((NUM_BUCKETS, DSA_HEADS), 0.2),
        "norm_mix": 1.0 + nrm((L, D_MODEL), 0.02),
        "w_in": nrm((L, D_MODEL, D_IN), D_MODEL ** -0.5),
        "gla_wa2": nrm((L, GLA_RANK, GLA_HEADS * GLA_DK), GLA_RANK ** -0.5),
        "gla_ba": nrm((L, GLA_HEADS * GLA_DK), 0.1),
        "gla_onorm": 1.0 + nrm((L, GLA_DV), 0.02),
        "dsa_qnorm": 1.0 + nrm((L, DSA_DH), 0.02),
        "dsa_knorm": 1.0 + nrm((L, DSA_DH), 0.02),
        "conv_w": nrm((L, CONV_WIDTH, CONV_CH), CONV_WIDTH ** -0.5),
        "conv_b": nrm((L, CONV_CH), 0.02),
        "conv_ln_g": 1.0 + nrm((L, CONV_CH), 0.02),
        "conv_ln_b": nrm((L, CONV_CH), 0.02),
        "w_out": nrm((L, D_MIX, D_MODEL), D_MIX ** -0.5),
        "norm_x": 1.0 + nrm((L, D_MODEL), 0.02),
        "norm_mem": 1.0 + nrm((L, D_MODEL), 0.02),
        "w_cq": nrm((L, D_MODEL, X_HEADS * X_DH), D_MODEL ** -0.5),
        "w_ckv": nrm((L, D_MODEL, 2 * X_HEADS * X_DH), D_MODEL ** -0.5),
        "x_qnorm": 1.0 + nrm((L, X_DH), 0.02),
        "x_knorm": 1.0 + nrm((L, X_DH), 0.02),
        "w_co": nrm((L, X_HEADS * X_DH, D_MODEL), (X_HEADS * X_DH) ** -0.5),
        "norm_ffn": 1.0 + nrm((L, D_MODEL), 0.02),
        "w_rg": nrm((L, D_MODEL, N_GROUPS), D_MODEL ** -0.5),
        "b_rg": nrm((L, N_GROUPS), 0.01),
        "w_re": nrm((L, D_MODEL, N_EXPERTS), D_MODEL ** -0.5),
        "b_re": nrm((L, N_EXPERTS), 0.01),
        "w_gate": nrm((L, N_EXPERTS, D_MODEL, D_EXPERT), D_MODEL ** -0.5),
        "w_up": nrm((L, N_EXPERTS, D_MODEL, D_EXPERT), D_MODEL ** -0.5),
        "w_down": nrm((L, N_EXPERTS, D_EXPERT, D_MODEL), D_EXPERT ** -0.5),
    }


def reference(x, mem, rel_bias, norm_mix, w_in, gla_wa2, gla_ba, gla_onorm, dsa_qnorm, dsa_knorm,
              conv_w, conv_b, conv_ln_g, conv_ln_b, w_out, norm_x, norm_mem, w_cq, w_ckv,
              x_qnorm, x_knorm, w_co, norm_ffn, w_rg, b_rg, w_re, b_re, w_gate, w_up, w_down):
    B, S, _ = x.shape
    topk = min(TOPK_MAX, S // 4)
    split_at = [int(o) for o in np.cumsum(IN_SIZES)[:-1]]
    for l in range(DEPTH):
        h = rms_norm(x, norm_mix[l])
        z = h @ w_in[l]
        (g_q, g_k, g_v, g_lr, g_r, d_q, d_k, d_v, i_q, i_k, i_w, c_u) = jnp.split(z, split_at, axis=-1)
        log_a = jax.nn.log_sigmoid((g_lr @ gla_wa2[l] + gla_ba[l]).astype(jnp.float32)) / GLA_TAU
        o_gla = gla_chunked(g_q.reshape(B, S, GLA_HEADS, GLA_DK), g_k.reshape(B, S, GLA_HEADS, GLA_DK),
                            g_v.reshape(B, S, GLA_HEADS, GLA_DV), log_a.reshape(B, S, GLA_HEADS, GLA_DK))
        o_gla = rms_norm(o_gla.astype(x.dtype), gla_onorm[l]) * jax.nn.silu(g_r.reshape(B, S, GLA_HEADS, GLA_DV))
        dq = rms_norm(d_q.reshape(B, S, DSA_HEADS, DSA_DH), dsa_qnorm[l])
        dk = rms_norm(d_k.reshape(B, S, DSA_HEADS, DSA_DH), dsa_knorm[l])
        o_dsa = dsa_attention(dq, dk, d_v.reshape(B, S, DSA_HEADS, DSA_DH),
                              i_q.reshape(B, S, IDX_HEADS, IDX_DIM), i_k, i_w * (IDX_HEADS ** -0.5),
                              rel_bias, topk)
        o_conv = conformer_conv(c_u, conv_w[l], conv_b[l], conv_ln_g[l], conv_ln_b[l])
        mix = jnp.concatenate([o_gla.reshape(B, S, GLA_WIDTH), o_dsa.reshape(B, S, DSA_WIDTH), o_conv], axis=-1)
        x = x + mix @ w_out[l]
        x = x + memory_cross_attention(rms_norm(x, norm_x[l]), rms_norm(mem, norm_mem[l]),
                                       w_cq[l], w_ckv[l], x_qnorm[l], x_knorm[l], w_co[l])
        x = x + hierarchical_moe(rms_norm(x, norm_ffn[l]), w_rg[l], b_rg[l], w_re[l], b_re[l],
                                 w_gate[l], w_up[l], w_down[l])
    return x
```

```python
import functools
import math

import jax
import jax.numpy as jnp
import numpy as np
from jax import lax
from jax.experimental import pallas as pl
from jax.experimental.pallas import tpu as pltpu

F32 = jnp.float32
BF16 = jnp.bfloat16
I32 = jnp.int32

LANES = 128
NORM_EPS = 1e-6

GLA_HEADS = 4
GLA_DK = 48
GLA_DV = 96
GLA_DKP = 64
GLA_RANK = 16
GLA_TAU = 16.0
DSA_HEADS = 6
DSA_DH = 64
IDX_HEADS = 8
IDX_DIM = 64
TOPK_MAX = 256
CONV_CH = 256
CONV_WIDTH = 31
NUM_BUCKETS = 32
MAX_DISTANCE = 128
X_HEADS = 4
X_DH = 64
N_GROUPS = 4
EXPERTS_PER_GROUP = 4
N_EXPERTS = 16

DSA_W = DSA_HEADS * LANES
GLA_VW = GLA_HEADS * GLA_DV
GLA_KW = GLA_HEADS * GLA_DKP
IQ_W = IDX_HEADS * IDX_DIM
CU_W = 2 * CONV_CH
OFF_DQ, OFF_DK, OFF_DV = 0, DSA_W, 2 * DSA_W
OFF_GV = 3 * DSA_W
OFF_GR = OFF_GV + GLA_VW
OFF_GQ = OFF_GR + GLA_VW
OFF_GK = OFF_GQ + GLA_KW
OFF_IQ = OFF_GK + GLA_KW
OFF_CU = OFF_IQ + IQ_W
OFF_LR = OFF_CU + CU_W
OFF_IK = OFF_LR + LANES
NP_IN = OFF_IK + LANES

NEG = -1e30
INT_MIN = -2 ** 31
INT_MAX = 2 ** 31 - 1

VMEM_LIMIT = 56 * 1024 * 1024


def _cparams(sem):
    return pltpu.CompilerParams(dimension_semantics=sem, vmem_limit_bytes=VMEM_LIMIT)


def _dot(a, b):
    return jnp.dot(a, b, preferred_element_type=F32)


def _dot_nt(a, b):
    return lax.dot_general(a, b, (((1,), (1,)), ((), ())), preferred_element_type=F32)


def _dot_hi(a, b):
    return jnp.dot(a, b, preferred_element_type=F32, precision=lax.Precision.HIGHEST)


def _rms(x, g):
    return x * lax.rsqrt(jnp.mean(x * x, axis=-1, keepdims=True) + NORM_EPS) * g


_IN_CHUNKS = ((OFF_DQ, DSA_W), (OFF_DK, DSA_W), (OFF_DV, DSA_W), (OFF_GV, 2 * GLA_VW),
              (OFF_GQ, 2 * GLA_KW), (OFF_IQ, IQ_W), (OFF_CU, CU_W), (OFF_LR, 2 * LANES))


def _in_proj_kernel(x_ref, g_ref, w_ref, qn_ref, kn_ref, ones_ref, o_ref):
    h = _rms(x_ref[...], g_ref[...]).astype(BF16)

    def head_norm(y, gain, scale):
        parts = []
        for hh in range(DSA_HEADS):
            yh = y[:, hh * LANES:(hh + 1) * LANES]
            ss = jnp.sum(yh * yh, axis=-1, keepdims=True) * (1.0 / DSA_DH)
            parts.append(yh * lax.rsqrt(ss + NORM_EPS) * gain * scale)
        return jnp.concatenate(parts, axis=-1)

    for c0, cw in _IN_CHUNKS:
        y = _dot(h, w_ref[:, c0:c0 + cw])
        if c0 == OFF_DQ:
            y = head_norm(y, qn_ref[...], DSA_DH ** -0.5)
        elif c0 == OFF_DK:
            y = head_norm(y, kn_ref[...], 1.0)
        elif c0 == OFF_DV:
            y = y + ones_ref[...]
        o_ref[:, c0:c0 + cw] = y.astype(BF16)


def _in_proj(x2d, g, w_p, qn_p, kn_p, ones_p, tm):
    T, D = x2d.shape
    return pl.pallas_call(
        _in_proj_kernel,
        out_shape=jax.ShapeDtypeStruct((T, NP_IN), BF16),
        grid=(T // tm,),
        in_specs=[
            pl.BlockSpec((tm, D), lambda i: (i, 0)),
            pl.BlockSpec((1, D), lambda i: (0, 0)),
            pl.BlockSpec((D, NP_IN), lambda i: (0, 0)),
            pl.BlockSpec((1, LANES), lambda i: (0, 0)),
            pl.BlockSpec((1, LANES), lambda i: (0, 0)),
            pl.BlockSpec((1, DSA_W), lambda i: (0, 0)),
        ],
        out_specs=pl.BlockSpec((tm, NP_IN), lambda i: (i, 0)),
        compiler_params=_cparams(("arbitrary",)),
        name="in_proj",
    )(x2d, g, w_p, qn_p, kn_p, ones_p)


GLA_C = 64
GLA_SUB = 16
GLA_NSUB = GLA_C // GLA_SUB


def _gla_kernel(q_ref, k_ref, v_ref, r_ref, lr_ref, wa2_ref, ba_ref, on_ref, e_ref, e2_ref,
                et_ref, tril_ref, o_ref, st_ref, qf_ref, kf_ref, vf_ref, bf_ref, of_ref, *, rows):
    c = pl.program_id(1)

    @pl.when(c == 0)
    def _():
        st_ref[...] = jnp.zeros_like(st_ref)

    pre = _dot_hi(lr_ref[...].astype(F32), wa2_ref[...]) + ba_ref[...]
    log_a = (jnp.minimum(pre, 0.0) - jnp.log(1.0 + jnp.exp(-jnp.abs(pre)))) * (1.0 / GLA_TAU)
    bf_ref[...] = _dot_hi(tril_ref[...], log_a)
    qf_ref[...] = q_ref[...].astype(F32) * (GLA_DK ** -0.5)
    kf_ref[...] = k_ref[...].astype(F32)
    vf_ref[...] = v_ref[...].astype(F32)
    e_mat = e_ref[...]
    row_id = lax.broadcasted_iota(I32, (GLA_SUB, GLA_KW), 0)

    def chunk(ci, carry):
        r0 = pl.multiple_of(ci * GLA_C, GLA_C)
        q = qf_ref[pl.ds(r0, GLA_C), :]
        k = kf_ref[pl.ds(r0, GLA_C), :]
        v = vf_ref[pl.ds(r0, GLA_C), :]
        b = bf_ref[pl.ds(r0, GLA_C), :]
        st = st_ref[...]
        o_inter = _dot_nt((q * jnp.exp(b)).astype(BF16), st.astype(BF16))
        b_last = b[GLA_C - 1:GLA_C, :]
        kd = (k * jnp.exp(b_last - b)).astype(BF16)
        upd = _dot(v.T.astype(BF16), kd) * et_ref[...]
        st_ref[...] = st * jnp.exp(b_last) + upd
        outs = []
        for i in range(GLA_NSUB):
            qi = q[i * GLA_SUB:(i + 1) * GLA_SUB, :]
            bi = b[i * GLA_SUB:(i + 1) * GLA_SUB, :]
            acc = o_inter[i * GLA_SUB:(i + 1) * GLA_SUB, :]
            for j in range(i + 1):
                slabs = []
                for s in range(GLA_SUB):
                    row = r0 + (j * GLA_SUB + s)
                    ks = kf_ref[pl.ds(row, 1), :]
                    bs = bf_ref[pl.ds(row, 1), :]
                    p = qi * ks * jnp.exp(jnp.minimum(bi - bs, 0.0))
                    if i == j:
                        p = jnp.where(row_id >= s, p, 0.0)
                    slabs.append(p.astype(BF16))
                pr = _dot(jnp.concatenate(slabs, axis=0), e_mat)
                for s in range(GLA_SUB):
                    vs = vf_ref[pl.ds(r0 + (j * GLA_SUB + s), 1), :]
                    acc = acc + pr[s * GLA_SUB:(s + 1) * GLA_SUB, :] * vs
            outs.append(acc)
        of_ref[pl.ds(r0, GLA_C), :] = jnp.concatenate(outs, axis=0)
        return carry

    lax.fori_loop(0, rows // GLA_C, chunk, 0)

    o = of_ref[...]
    ss = _dot_hi(o * o, e2_ref[...]) * (1.0 / GLA_DV)
    o = o * lax.rsqrt(ss + NORM_EPS) * on_ref[...]
    r = r_ref[...].astype(F32)
    o_ref[...] = (o * (r / (1.0 + jnp.exp(-r)))).astype(BF16)


def _gla(z, wa2_p, ba_p, on_p, B, S, rows):
    T = z.shape[0]
    nblk = S // rows
    e = np.zeros((GLA_KW, GLA_VW), np.float32)
    for h in range(GLA_HEADS):
        e[h * GLA_DKP:(h + 1) * GLA_DKP, h * GLA_DV:(h + 1) * GLA_DV] = 1.0
    e2 = np.zeros((GLA_VW, GLA_VW), np.float32)
    for h in range(GLA_HEADS):
        e2[h * GLA_DV:(h + 1) * GLA_DV, h * GLA_DV:(h + 1) * GLA_DV] = 1.0
    tr = np.zeros((rows, rows), np.float32)
    for cc in range(rows // GLA_C):
        tr[cc * GLA_C:(cc + 1) * GLA_C, cc * GLA_C:(cc + 1) * GLA_C] = np.tril(np.ones((GLA_C, GLA_C)))
    rowmap = lambda b, c: (b * nblk + c)
    const = lambda b, c: (0, 0)
    return pl.pallas_call(
        functools.partial(_gla_kernel, rows=rows),
        out_shape=jax.ShapeDtypeStruct((T, GLA_VW), BF16),
        grid=(B, nblk),
        in_specs=[
            pl.BlockSpec((rows, GLA_KW), lambda b, c: (rowmap(b, c), OFF_GQ // GLA_KW)),
            pl.BlockSpec((rows, GLA_KW), lambda b, c: (rowmap(b, c), OFF_GK // GLA_KW)),
            pl.BlockSpec((rows, GLA_VW), lambda b, c: (rowmap(b, c), OFF_GV // GLA_VW)),
            pl.BlockSpec((rows, GLA_VW), lambda b, c: (rowmap(b, c), OFF_GR // GLA_VW)),
            pl.BlockSpec((rows, LANES), lambda b, c: (rowmap(b, c), OFF_LR // LANES)),
            pl.BlockSpec((LANES, GLA_KW), const),
            pl.BlockSpec((1, GLA_KW), const),
            pl.BlockSpec((1, GLA_VW), const),
            pl.BlockSpec((GLA_KW, GLA_VW), const),
            pl.BlockSpec((GLA_VW, GLA_VW), const),
            pl.BlockSpec((GLA_VW, GLA_KW), const),
            pl.BlockSpec((rows, rows), const),
        ],
        out_specs=pl.BlockSpec((rows, GLA_VW), lambda b, c: (rowmap(b, c), 0)),
        scratch_shapes=[
            pltpu.VMEM((GLA_VW, GLA_KW), F32),
            pltpu.VMEM((rows, GLA_KW), F32),
            pltpu.VMEM((rows, GLA_KW), F32),
            pltpu.VMEM((rows, GLA_VW), F32),
            pltpu.VMEM((rows, GLA_KW), F32),
            pltpu.VMEM((rows, GLA_VW), F32),
        ],
        compiler_params=_cparams(("arbitrary", "arbitrary")),
        name="gla",
    )(z, z, z, z, z, wa2_p, ba_p, on_p, jnp.asarray(e, BF16), jnp.asarray(e2), jnp.asarray(e.T),
      jnp.asarray(tr))


DSA_TQ = 256


def _t5_bucket(dist):
    max_exact = NUM_BUCKETS // 2
    d = jnp.maximum(dist, 1).astype(F32)
    large = max_exact + (jnp.log(d / max_exact) / math.log(MAX_DISTANCE / max_exact)
                         * (NUM_BUCKETS - max_exact)).astype(I32)
    large = jnp.minimum(large, NUM_BUCKETS - 1)
    return jnp.where(dist < max_exact, dist, large)


def _dsa_kernel(rb_ref, dq_ref, dk_ref, dv_ref, iq_ref, ik_ref, iw_ref, o_ref,
                bias_ref, key_ref, iqs_ref, wb_ref, cnt_ref, lo_ref, hi_ref, clo_ref, chi_ref,
                done_ref, m_ref, acc_ref, *, seq, topk):
    TQ = DSA_TQ
    bi = pl.program_id(0)
    qb = pl.program_id(1)
    nkb = qb + 1
    log2_seq = int(math.log2(seq))

    ti = lax.broadcasted_iota(I32, (TQ, TQ), 0)
    si = lax.broadcasted_iota(I32, (TQ, TQ), 1)

    @pl.when((bi == 0) & (qb == 0))
    def _():
        for delta in range(3):
            bucket = _t5_bucket(jnp.maximum(delta * TQ + ti - si, 0))
            for hh in range(DSA_HEADS):
                tile = jnp.zeros((TQ, TQ), F32)
                for bk in range(NUM_BUCKETS):
                    tile = jnp.where(bucket == bk, rb_ref[bk, hh], tile)
                bias_ref[hh, delta] = tile

    for hh in range(IDX_HEADS):
        iqs_ref[hh * TQ:(hh + 1) * TQ, :] = iq_ref[:, hh * IDX_DIM:(hh + 1) * IDX_DIM]
        w = iw_ref[:, IDX_DIM + hh:IDX_DIM + hh + 1].astype(F32)
        w = w * (IDX_HEADS ** -0.5) * (IDX_DIM ** -0.5)
        wb_ref[hh] = jnp.broadcast_to(w, (TQ, TQ))

    def score_block(kb, carry):
        k0 = pl.multiple_of(kb * TQ, TQ)
        ik = ik_ref[pl.ds(k0, TQ), :][:, :IDX_DIM]
        dots = _dot_nt(iqs_ref[...], ik)
        sc = jnp.zeros((TQ, TQ), F32)
        for hh in range(IDX_HEADS):
            sc = sc + jnp.maximum(dots[hh * TQ:(hh + 1) * TQ, :], 0.0) * wb_ref[hh]
        bits = pltpu.bitcast(sc, I32)
        keys = bits ^ ((bits >> 31) & INT_MAX)
        causal = (k0 + si) <= (qb * TQ + ti)
        key_ref[kb] = jnp.where(causal, keys, INT_MIN)
        return carry

    lax.fori_loop(0, nkb, score_block, 0)

    tpos = qb * TQ + lax.broadcasted_iota(I32, (TQ, 1), 0)
    ncausal = tpos + 1
    lo_ref[...] = jnp.full((TQ, 1), INT_MIN + 1, I32)
    hi_ref[...] = jnp.full((TQ, 1), INT_MAX, I32)
    clo_ref[...] = ncausal
    chi_ref[...] = jnp.zeros((TQ, 1), I32)
    done_ref[...] = (ncausal <= topk).astype(I32)

    def count_rows(pred):
        cnt_ref[...] = jnp.zeros_like(cnt_ref)

        def body(kb, carry):
            hit = jnp.where(pred(key_ref[kb], kb), 1, 0)
            cnt_ref[...] += hit[:, :LANES] + hit[:, LANES:]
            return carry

        lax.fori_loop(0, nkb, body, 0)
        return jnp.sum(cnt_ref[...], axis=1, keepdims=True)

    def bis_cond(state):
        it, active = state
        return (it < 34) & (active > 0)

    def bis_body(state):
        it, _ = state
        lo = lo_ref[...]
        hi = hi_ref[...]
        done = done_ref[...]
        mid = (lo >> 1) + (hi >> 1) + (lo & hi & 1)
        mid_b = jnp.broadcast_to(mid, (TQ, TQ))
        c = count_rows(lambda kk, kb: kk >= mid_b)
        live = done == 0
        conv = mid == lo
        ge = c >= topk
        take_lo = live & ge & jnp.logical_not(conv)
        take_hi = live & jnp.logical_not(ge)
        lo_ref[...] = jnp.where(take_lo, mid, lo)
        clo_ref[...] = jnp.where(take_lo, c, clo_ref[...])
        hi_ref[...] = jnp.where(take_hi, mid, hi)
        chi_ref[...] = jnp.where(take_hi, c, chi_ref[...])
        new_done = jnp.where(live & (conv | (c == topk)), 1, done)
        done_ref[...] = new_done
        return it + 1, jnp.sum(1 - new_done)

    lax.while_loop(bis_cond, bis_body, (jnp.int32(0), jnp.sum(1 - done_ref[...])))

    tie_rows = (clo_ref[...] > topk) & (ncausal > topk)

    @pl.when(jnp.sum(tie_rows.astype(I32)) > 0)
    def _():
        thr_b = jnp.broadcast_to(lo_ref[...], (TQ, TQ))
        need = topk - chi_ref[...]
        plo = jnp.full((TQ, 1), -1, I32)
        phi = jnp.full((TQ, 1), seq - 1, I32)
        for _ in range(log2_seq):
            pm = (plo + phi) >> 1
            pm_b = jnp.broadcast_to(pm, (TQ, TQ))
            c = count_rows(lambda kk, kb: (kk == thr_b) & ((kb * TQ + si) <= pm_b))
            ok = c >= need
            phi = jnp.where(ok, pm, phi)
            plo = jnp.where(ok, plo, pm)
        cut_b = jnp.broadcast_to(jnp.where(tie_rows, phi, seq - 1), (TQ, TQ))

        def demote(kb, carry):
            kk = key_ref[kb]
            drop = (kk == thr_b) & ((kb * TQ + si) > cut_b)
            key_ref[kb] = jnp.where(drop, INT_MIN, kk)
            return carry

        lax.fori_loop(0, nkb, demote, 0)

    m_ref[...] = jnp.full(m_ref.shape, NEG, F32)
    acc_ref[...] = jnp.zeros_like(acc_ref)
    thr_sel = jnp.broadcast_to(lo_ref[...], (TQ, TQ))

    def attend(kb, carry):
        k0 = pl.multiple_of(kb * TQ, TQ)
        sel = key_ref[kb] >= thr_sel
        delta = jnp.minimum(qb - kb, 2)
        for hh in range(DSA_HEADS):
            qh = dq_ref[:, hh * LANES:(hh + 1) * LANES]
            kh = dk_ref[pl.ds(k0, TQ), hh * LANES:(hh + 1) * LANES]
            logits = _dot_nt(qh, kh) + bias_ref[hh, delta]
            logits = jnp.where(sel, logits, NEG)
            m_prev = m_ref[hh]
            m_new = jnp.maximum(m_prev, jnp.max(logits, axis=1, keepdims=True))
            p = jnp.exp(logits - m_new)
            vh = dv_ref[pl.ds(k0, TQ), hh * LANES:(hh + 1) * LANES]
            acc_ref[hh] = acc_ref[hh] * jnp.exp(m_prev - m_new) + _dot(p.astype(BF16), vh)
            m_ref[hh] = m_new
        return carry

    lax.fori_loop(0, nkb, attend, 0)

    outs = []
    for hh in range(DSA_HEADS):
        a = acc_ref[hh]
        outs.append(a[:, :DSA_DH] / a[:, DSA_DH:DSA_DH + 1])
    o_ref[...] = jnp.concatenate(outs, axis=-1).astype(BF16)


def _dsa(z, rel_bias, B, S):
    T = z.shape[0]
    TQ = DSA_TQ
    nqb = S // TQ
    topk = min(TOPK_MAX, S // 4)
    qrow = lambda b, q: b * nqb + q
    return pl.pallas_call(
        functools.partial(_dsa_kernel, seq=S, topk=topk),
        out_shape=jax.ShapeDtypeStruct((T, DSA_HEADS * DSA_DH), BF16),
        grid=(B, nqb),
        in_specs=[
            pl.BlockSpec(memory_space=pltpu.SMEM),
            pl.BlockSpec((TQ, DSA_W), lambda b, q: (qrow(b, q), OFF_DQ // DSA_W)),
            pl.BlockSpec((S, DSA_W), lambda b, q: (b, OFF_DK // DSA_W)),
            pl.BlockSpec((S, DSA_W), lambda b, q: (b, OFF_DV // DSA_W)),
            pl.BlockSpec((TQ, IQ_W), lambda b, q: (qrow(b, q), OFF_IQ // IQ_W)),
            pl.BlockSpec((S, LANES), lambda b, q: (b, OFF_IK // LANES)),
            pl.BlockSpec((TQ, LANES), lambda b, q: (qrow(b, q), OFF_IK // LANES)),
        ],
        out_specs=pl.BlockSpec((TQ, DSA_HEADS * DSA_DH), lambda b, q: (qrow(b, q), 0)),
        scratch_shapes=[
            pltpu.VMEM((DSA_HEADS, 3, TQ, TQ), F32),
            pltpu.VMEM((S // TQ, TQ, TQ), I32),
            pltpu.VMEM((IDX_HEADS * TQ, IDX_DIM), BF16),
            pltpu.VMEM((IDX_HEADS, TQ, TQ), F32),
            pltpu.VMEM((TQ, LANES), I32),
            pltpu.VMEM((TQ, 1), I32),
            pltpu.VMEM((TQ, 1), I32),
            pltpu.VMEM((TQ, 1), I32),
            pltpu.VMEM((TQ, 1), I32),
            pltpu.VMEM((TQ, 1), I32),
            pltpu.VMEM((DSA_HEADS, TQ, 1), F32),
            pltpu.VMEM((DSA_HEADS, TQ, LANES), F32),
        ],
        compiler_params=_cparams(("arbitrary", "arbitrary")),
        name="dsa",
    )(rel_bias, z, z, z, z, z, z)


CONV_HALO = 32


def _conv_kernel(u_ref, w_ref, cb_ref, g_ref, b_ref, o_ref, h_ref, *, tm):
    @pl.when(pl.program_id(1) == 0)
    def _():
        h_ref[0:CONV_HALO, :] = jnp.zeros((CONV_HALO, CONV_CH), F32)

    u = u_ref[...].astype(F32)
    a = u[:, :CONV_CH]
    g = u[:, CONV_CH:]
    h_ref[CONV_HALO:CONV_HALO + tm, :] = a / (1.0 + jnp.exp(-g))
    acc = jnp.zeros((tm, CONV_CH), F32) + cb_ref[...]
    base = CONV_HALO - (CONV_WIDTH - 1)
    for j in range(CONV_WIDTH):
        acc = acc + h_ref[base + j:base + j + tm, :] * w_ref[j:j + 1, :]
    h_ref[0:CONV_HALO, :] = h_ref[tm:tm + CONV_HALO, :]
    mu = jnp.mean(acc, axis=-1, keepdims=True)
    d = acc - mu
    var = jnp.mean(d * d, axis=-1, keepdims=True)
    y = d * lax.rsqrt(var + NORM_EPS) * g_ref[...] + b_ref[...]
    o_ref[...] = (y / (1.0 + jnp.exp(-y))).astype(BF16)


def _conv(z, w, cb, g, b, B, S, tm):
    T = z.shape[0]
    nblk = S // tm
    const = lambda bb, c: (0, 0)
    return pl.pallas_call(
        functools.partial(_conv_kernel, tm=tm),
        out_shape=jax.ShapeDtypeStruct((T, CONV_CH), BF16),
        grid=(B, nblk),
        in_specs=[
            pl.BlockSpec((tm, CU_W), lambda bb, c: (bb * nblk + c, OFF_CU // CU_W)),
            pl.BlockSpec((CONV_WIDTH, CONV_CH), const),
            pl.BlockSpec((1, CONV_CH), const),
            pl.BlockSpec((1, CONV_CH), const),
            pl.BlockSpec((1, CONV_CH), const),
        ],
        out_specs=pl.BlockSpec((tm, CONV_CH), lambda bb, c: (bb * nblk + c, 0)),
        scratch_shapes=[pltpu.VMEM((tm + CONV_HALO, CONV_CH), F32)],
        compiler_params=_cparams(("arbitrary", "arbitrary")),
        name="conv",
    )(z, w, cb, g, b)


XW = X_HEADS * X_DH


def _mem_kv_kernel(m_ref, g_ref, w_ref, kn_ref, e_ref, k_ref, v_ref, *, mlen):
    hm = _rms(m_ref[...], g_ref[...]).astype(BF16)
    kv = _dot(hm, w_ref[...])
    k = kv[:, :XW]
    v = kv[:, XW:]
    ss = _dot_hi(k * k, e_ref[...]) * (1.0 / X_DH)
    k = k * lax.rsqrt(ss + NORM_EPS) * kn_ref[...] * (X_DH ** -0.5)
    lane_head = lax.broadcasted_iota(I32, (mlen, XW), 1) // X_DH
    for hh in range(X_HEADS):
        msk = lane_head == hh
        k_ref[hh * mlen:(hh + 1) * mlen, :] = jnp.where(msk, k, 0.0).astype(BF16)
        v_ref[hh * mlen:(hh + 1) * mlen, :] = jnp.where(msk, v, 0.0).astype(BF16)


def _mem_kv(mem2d, g, w_ckv, kn_t, e_x, B, M):
    D = mem2d.shape[1]
    const = lambda b: (0, 0)
    out = jax.ShapeDtypeStruct((B * X_HEADS * M, XW), BF16)
    return pl.pallas_call(
        functools.partial(_mem_kv_kernel, mlen=M),
        out_shape=(out, out),
        grid=(B,),
        in_specs=[
            pl.BlockSpec((M, D), lambda b: (b, 0)),
            pl.BlockSpec((1, D), const),
            pl.BlockSpec((D, 2 * XW), const),
            pl.BlockSpec((1, XW), const),
            pl.BlockSpec((XW, XW), const),
        ],
        out_specs=(pl.BlockSpec((X_HEADS * M, XW), lambda b: (b, 0)),
                   pl.BlockSpec((X_HEADS * M, XW), lambda b: (b, 0))),
        compiler_params=_cparams(("arbitrary",)),
        name="mem_kv",
    )(mem2d, g, w_ckv, kn_t, e_x)


def _post_kernel(x_ref, og_ref, od_ref, oc_ref, wo_ref, nx_ref, wcq_ref, qn_ref, e_ref,
                 kbd_ref, vbd_ref, obd_ref, wco_ref, nf_ref, wr_ref, br_ref,
                 x2_ref, hf_ref, gate_ref, *, mlen):
    wo = wo_ref
    x1 = (x_ref[...]
          + _dot(og_ref[...], wo[0:GLA_VW, :])
          + _dot(od_ref[...], wo[GLA_VW:2 * GLA_VW, :])
          + _dot(oc_ref[...], wo[2 * GLA_VW:, :]))
    hx = _rms(x1, nx_ref[...]).astype(BF16)
    q = _dot(hx, wcq_ref[...])
    ss = _dot_hi(q * q, e_ref[...]) * (1.0 / X_DH)
    q = (q * lax.rsqrt(ss + NORM_EPS) * qn_ref[...]).astype(BF16)
    logits = _dot_nt(q, kbd_ref[...])
    ps = []
    for hh in range(X_HEADS):
        lg = logits[:, hh * mlen:(hh + 1) * mlen]
        ps.append(jnp.exp(lg - jnp.max(lg, axis=-1, keepdims=True)))
    p = jnp.concatenate(ps, axis=-1).astype(BF16)
    o = _dot(p, vbd_ref[...]) / _dot(p, obd_ref[...])
    x2 = x1 + _dot(o.astype(BF16), wco_ref[...])
    x2_ref[...] = x2
    hf = _rms(x2, nf_ref[...])
    hf_ref[...] = hf.astype(BF16)
    rl = _dot_hi(hf, wr_ref[...]) + br_ref[...]
    tm = rl.shape[0]
    lane = lax.broadcasted_iota(I32, (tm, LANES), 1)
    is_g = (lane >= N_EXPERTS) & (lane < N_EXPERTS + N_GROUPS)
    g_max = jnp.max(jnp.where(is_g, rl, NEG), axis=-1, keepdims=True)
    g_sel = jnp.min(jnp.where(is_g & (rl == g_max), lane, LANES), axis=-1, keepdims=True) - N_EXPERTS
    g_den = jnp.sum(jnp.where(is_g, jnp.exp(rl - g_max), 0.0), axis=-1, keepdims=True)
    g_w = 1.0 / g_den
    in_grp = (lane < N_EXPERTS) & ((lane // EXPERTS_PER_GROUP) == g_sel)
    e1 = jnp.max(jnp.where(in_grp, rl, NEG), axis=-1, keepdims=True)
    l1 = jnp.min(jnp.where(in_grp & (rl == e1), lane, LANES), axis=-1, keepdims=True)
    rest = in_grp & (lane != l1)
    e2 = jnp.max(jnp.where(rest, rl, NEG), axis=-1, keepdims=True)
    l2 = jnp.min(jnp.where(rest & (rl == e2), lane, LANES), axis=-1, keepdims=True)
    p2 = jnp.exp(e2 - e1)
    w1 = 1.0 / (1.0 + p2)
    w2 = p2 / (1.0 + p2)
    gate_ref[...] = jnp.where(lane == l1, w1 * g_w, jnp.where(lane == l2, w2 * g_w, 0.0))


def _post(x2d, o_gla, o_dsa, o_conv, w_out, nx, w_cq, qn_t, e_x, kbd, vbd, obd, w_co, nf, w_r, b_r,
          B, S, M, tm):
    T, D = x2d.shape
    nblk = S // tm
    const = lambda i: (0, 0)
    row = lambda i: (i, 0)
    bat = lambda i: (i // nblk, 0)
    HM = X_HEADS * M
    return pl.pallas_call(
        functools.partial(_post_kernel, mlen=M),
        out_shape=(jax.ShapeDtypeStruct((T, D), F32), jax.ShapeDtypeStruct((T, D), BF16),
                   jax.ShapeDtypeStruct((T, LANES), F32)),
        grid=(T // tm,),
        in_specs=[
            pl.BlockSpec((tm, D), row),
            pl.BlockSpec((tm, GLA_VW), row),
            pl.BlockSpec((tm, DSA_HEADS * DSA_DH), row),
            pl.BlockSpec((tm, CONV_CH), row),
            pl.BlockSpec((D, D), const),
            pl.BlockSpec((1, D), const),
            pl.BlockSpec((D, XW), const),
            pl.BlockSpec((1, XW), const),
            pl.BlockSpec((XW, XW), const),
            pl.BlockSpec((HM, XW), bat),
            pl.BlockSpec((HM, XW), bat),
            pl.BlockSpec((HM, XW), const),
            pl.BlockSpec((XW, D), const),
            pl.BlockSpec((1, D), const),
            pl.BlockSpec((D, LANES), const),
            pl.BlockSpec((1, LANES), const),
        ],
        out_specs=(pl.BlockSpec((tm, D), row), pl.BlockSpec((tm, D), row),
                   pl.BlockSpec((tm, LANES), row)),
        compiler_params=_cparams(("arbitrary",)),
        name="post",
    )(x2d, o_gla, o_dsa, o_conv, w_out, nx, w_cq, qn_t, e_x, kbd, vbd, obd, w_co, nf, w_r, b_r)


def _moe_kernel(x2_ref, hf_ref, gate_ref, wg_ref, wu_ref, wd_ref, o_ref):
    e = pl.program_id(1)

    @pl.when(e == 0)
    def _():
        o_ref[...] = x2_ref[...]

    t = hf_ref[...]
    lane = lax.broadcasted_iota(I32, gate_ref.shape, 1)
    g = jnp.sum(jnp.where(lane == e, gate_ref[...], 0.0), axis=-1, keepdims=True)
    a = _dot(t, wg_ref[0])
    u = _dot(t, wu_ref[0])
    he = (a / (1.0 + jnp.exp(-a))) * u * g
    o_ref[...] += _dot(he.astype(BF16), wd_ref[0])


def _moe(x2, hf, gates, wg, wu, wd, tm):
    T, D = x2.shape
    DE = wg.shape[-1]
    row = lambda i, e: (i, 0)
    return pl.pallas_call(
        _moe_kernel,
        out_shape=jax.ShapeDtypeStruct((T, D), F32),
        grid=(T // tm, N_EXPERTS),
        in_specs=[
            pl.BlockSpec((tm, D), row),
            pl.BlockSpec((tm, D), row),
            pl.BlockSpec((tm, LANES), row),
            pl.BlockSpec((1, D, DE), lambda i, e: (e, 0, 0)),
            pl.BlockSpec((1, D, DE), lambda i, e: (e, 0, 0)),
            pl.BlockSpec((1, DE, D), lambda i, e: (e, 0, 0)),
        ],
        out_specs=pl.BlockSpec((tm, D), row),
        compiler_params=_cparams(("arbitrary", "arbitrary")),
        name="moe",
    )(x2, hf, gates, wg, wu, wd)


def _pad_heads(w, heads, dh, slot):
    lead = w.shape[:-1]
    w = w.reshape(lead + (heads, dh))
    w = jnp.pad(w, [(0, 0)] * len(lead) + [(0, 0), (0, slot - dh)])
    return w.reshape(lead + (heads * slot,))


def _pad_cols(w, width):
    return jnp.pad(w, [(0, 0)] * (w.ndim - 1) + [(0, width - w.shape[-1])])


def _layout_w_in(w):
    sizes = (GLA_HEADS * GLA_DK, GLA_HEADS * GLA_DK, GLA_VW, GLA_RANK, GLA_VW,
             DSA_HEADS * DSA_DH, DSA_HEADS * DSA_DH, DSA_HEADS * DSA_DH, IQ_W, IDX_DIM, IDX_HEADS,
             CU_W)
    offs = np.cumsum((0,) + sizes)
    g_q, g_k, g_v, g_lr, g_r, d_q, d_k, d_v, i_q, i_k, i_w, c_u = [
        w[:, offs[n]:offs[n + 1]] for n in range(len(sizes))]
    cols = [
        _pad_heads(d_q, DSA_HEADS, DSA_DH, LANES),
        _pad_heads(d_k, DSA_HEADS, DSA_DH, LANES),
        _pad_heads(d_v, DSA_HEADS, DSA_DH, LANES),
        g_v, g_r,
        _pad_heads(g_q, GLA_HEADS, GLA_DK, GLA_DKP),
        _pad_heads(g_k, GLA_HEADS, GLA_DK, GLA_DKP),
        i_q, c_u,
        _pad_cols(g_lr, LANES),
        _pad_cols(jnp.concatenate([i_k, i_w], axis=-1), LANES),
    ]
    out = jnp.concatenate(cols, axis=-1)
    assert out.shape[-1] == NP_IN
    return out.astype(BF16)


def _block_ones(n, blk):
    idx = np.arange(n) // blk
    return jnp.asarray((idx[:, None] == idx[None, :]).astype(np.float32))


def _layer(x2d, mem2d, rel_bias, p, B, S, M):
    D = x2d.shape[1]
    row = lambda v: v.reshape(1, -1)
    ones_dv = np.zeros((1, DSA_W), np.float32)
    ones_dv[0, DSA_DH::LANES] = 1.0
    z = _in_proj(x2d, row(p["norm_mix"]), _layout_w_in(p["w_in"]),
                 _pad_cols(row(p["dsa_qnorm"]), LANES), _pad_cols(row(p["dsa_knorm"]), LANES),
                 jnp.asarray(ones_dv), tm=min(512, S))
    wa2_p = jnp.pad(_pad_heads(p["gla_wa2"], GLA_HEADS, GLA_DK, GLA_DKP),
                    ((0, LANES - GLA_RANK), (0, 0)))
    o_gla = _gla(z, wa2_p, row(_pad_heads(p["gla_ba"], GLA_HEADS, GLA_DK, GLA_DKP)),
                 row(jnp.tile(p["gla_onorm"], GLA_HEADS)), B, S, rows=min(256, S))
    o_dsa = _dsa(z, rel_bias, B, S)
    o_conv = _conv(z, p["conv_w"], row(p["conv_b"]), row(p["conv_ln_g"]), row(p["conv_ln_b"]),
                   B, S, tm=min(512, S))
    e_x = _block_ones(XW, X_DH)
    kbd, vbd = _mem_kv(mem2d, row(p["norm_mem"]), p["w_ckv"].astype(BF16),
                       row(jnp.tile(p["x_knorm"], X_HEADS)), e_x, B, M)
    obd = (jnp.arange(X_HEADS * M)[:, None] // M == jnp.arange(XW)[None, :] // X_DH).astype(BF16)
    w_r = _pad_cols(jnp.concatenate([p["w_re"], p["w_rg"]], axis=-1), LANES)
    b_r = _pad_cols(row(jnp.concatenate([p["b_re"], p["b_rg"]])), LANES)
    x2, hf, gates = _post(
        x2d, o_gla, o_dsa, o_conv, p["w_out"].astype(BF16), row(p["norm_x"]),
        p["w_cq"].astype(BF16), row(jnp.tile(p["x_qnorm"], X_HEADS)), e_x, kbd, vbd, obd,
        p["w_co"].astype(BF16), row(p["norm_ffn"]), w_r, b_r, B, S, M, tm=min(256, S))
    return _moe(x2, hf, gates, p["w_gate"].astype(BF16), p["w_up"].astype(BF16),
                p["w_down"].astype(BF16), tm=min(1024, S))


_LAYER_KEYS = ("norm_mix", "w_in", "gla_wa2", "gla_ba", "gla_onorm", "dsa_qnorm", "dsa_knorm",
               "conv_w", "conv_b", "conv_ln_g", "conv_ln_b", "w_out", "norm_x", "norm_mem", "w_cq",
               "w_ckv", "x_qnorm", "x_knorm", "w_co", "norm_ffn", "w_rg", "b_rg", "w_re", "b_re",
               "w_gate", "w_up", "w_down")


def kernel(x, mem, rel_bias, norm_mix, w_in, gla_wa2, gla_ba, gla_onorm, dsa_qnorm, dsa_knorm,
           conv_w, conv_b, conv_ln_g, conv_ln_b, w_out, norm_x, norm_mem, w_cq, w_ckv, x_qnorm,
           x_knorm, w_co, norm_ffn, w_rg, b_rg, w_re, b_re, w_gate, w_up, w_down):
    stacked = dict(zip(_LAYER_KEYS, (
        norm_mix, w_in, gla_wa2, gla_ba, gla_onorm, dsa_qnorm, dsa_knorm, conv_w, conv_b,
        conv_ln_g, conv_ln_b, w_out, norm_x, norm_mem, w_cq, w_ckv, x_qnorm, x_knorm, w_co,
        norm_ffn, w_rg, b_rg, w_re, b_re, w_gate, w_up, w_down)))
    B, S, D = x.shape
    M = mem.shape[1]
    x2d = x.reshape(B * S, D)
    mem2d = mem.reshape(B * M, D)
    for l in range(norm_mix.shape[0]):
        x2d = _layer(x2d, mem2d, rel_bias, {k: v[l] for k, v in stacked.items()}, B, S, M)
    return x2d.reshape(B, S, D)
```

```python
import functools
import math

import jax
import jax.numpy as jnp
import numpy as np
from jax import lax
from jax.experimental import pallas as pl
from jax.experimental.pallas import tpu as pltpu

F32 = jnp.float32
BF16 = jnp.bfloat16
I32 = jnp.int32

LANES = 128
NORM_EPS = 1e-6

GLA_HEADS = 4
GLA_DK = 48
GLA_DV = 96
GLA_DKP = 64
GLA_RANK = 16
GLA_TAU = 16.0
DSA_HEADS = 6
DSA_DH = 64
IDX_HEADS = 8
IDX_DIM = 64
TOPK_MAX = 256
CONV_CH = 256
CONV_WIDTH = 31
NUM_BUCKETS = 32
MAX_DISTANCE = 128
X_HEADS = 4
X_DH = 64
N_GROUPS = 4
EXPERTS_PER_GROUP = 4
N_EXPERTS = 16

DSA_W = DSA_HEADS * LANES
GLA_VW = GLA_HEADS * GLA_DV
GLA_KW = GLA_HEADS * GLA_DKP
IQ_W = IDX_HEADS * IDX_DIM
CU_W = 2 * CONV_CH
OFF_DQ, OFF_DK, OFF_DV = 0, DSA_W, 2 * DSA_W
OFF_GV = 3 * DSA_W
OFF_GR = OFF_GV + GLA_VW
OFF_GQ = OFF_GR + GLA_VW
OFF_GK = OFF_GQ + GLA_KW
OFF_IQ = OFF_GK + GLA_KW
OFF_CU = OFF_IQ + IQ_W
OFF_LR = OFF_CU + CU_W
OFF_IK = OFF_LR + LANES
NP_IN = OFF_IK + LANES

NEG = -1e30
INT_MIN = -2 ** 31
INT_MAX = 2 ** 31 - 1

VMEM_LIMIT = 56 * 1024 * 1024


def _cparams(sem):
    return pltpu.CompilerParams(dimension_semantics=sem, vmem_limit_bytes=VMEM_LIMIT)


def _dot(a, b):
    return jnp.dot(a, b, preferred_element_type=F32)


def _dot_nt(a, b):
    return lax.dot_general(a, b, (((1,), (1,)), ((), ())), preferred_element_type=F32)


def _dot_hi(a, b):
    return jnp.dot(a, b, preferred_element_type=F32, precision=lax.Precision.HIGHEST)


def _rms(x, g):
    return x * lax.rsqrt(jnp.mean(x * x, axis=-1, keepdims=True) + NORM_EPS) * g


_IN_CHUNKS = ((OFF_DQ, DSA_W), (OFF_DK, DSA_W), (OFF_DV, DSA_W), (OFF_GV, 2 * GLA_VW),
              (OFF_GQ, 2 * GLA_KW), (OFF_IQ, IQ_W), (OFF_CU, CU_W), (OFF_LR, 2 * LANES))


def _in_proj_kernel(x_ref, g_ref, w_ref, qn_ref, kn_ref, ones_ref, o_ref):
    h = _rms(x_ref[...], g_ref[...]).astype(BF16)

    def head_norm(y, gain, scale):
        parts = []
        for hh in range(DSA_HEADS):
            yh = y[:, hh * LANES:(hh + 1) * LANES]
            ss = jnp.sum(yh * yh, axis=-1, keepdims=True) * (1.0 / DSA_DH)
            parts.append(yh * lax.rsqrt(ss + NORM_EPS) * gain * scale)
        return jnp.concatenate(parts, axis=-1)

    for c0, cw in _IN_CHUNKS:
        y = _dot(h, w_ref[:, c0:c0 + cw])
        if c0 == OFF_DQ:
            y = head_norm(y, qn_ref[...], DSA_DH ** -0.5)
        elif c0 == OFF_DK:
            y = head_norm(y, kn_ref[...], 1.0)
        elif c0 == OFF_DV:
            y = y + ones_ref[...]
        o_ref[:, c0:c0 + cw] = y.astype(BF16)


def _in_proj(x2d, g, w_p, qn_p, kn_p, ones_p, tm):
    T, D = x2d.shape
    return pl.pallas_call(
        _in_proj_kernel,
        out_shape=jax.ShapeDtypeStruct((T, NP_IN), BF16),
        grid=(T // tm,),
        in_specs=[
            pl.BlockSpec((tm, D), lambda i: (i, 0)),
            pl.BlockSpec((1, D), lambda i: (0, 0)),
            pl.BlockSpec((D, NP_IN), lambda i: (0, 0)),
            pl.BlockSpec((1, LANES), lambda i: (0, 0)),
            pl.BlockSpec((1, LANES), lambda i: (0, 0)),
            pl.BlockSpec((1, DSA_W), lambda i: (0, 0)),
        ],
        out_specs=pl.BlockSpec((tm, NP_IN), lambda i: (i, 0)),
        compiler_params=_cparams(("arbitrary",)),
        name="in_proj",
    )(x2d, g, w_p, qn_p, kn_p, ones_p)


GLA_C = 64
GLA_SUB = 16
GLA_NSUB = GLA_C // GLA_SUB


def _gla_kernel(q_ref, k_ref, v_ref, r_ref, lr_ref, wa2_ref, ba_ref, on_ref, e_ref, e2_ref,
                et_ref, tril_ref, o_ref, st_ref, qf_ref, kf_ref, vf_ref, bf_ref, of_ref, *, rows):
    c = pl.program_id(1)

    @pl.when(c == 0)
    def _():
        st_ref[...] = jnp.zeros_like(st_ref)

    pre = _dot_hi(lr_ref[...].astype(F32), wa2_ref[...]) + ba_ref[...]
    log_a = (jnp.minimum(pre, 0.0) - jnp.log(1.0 + jnp.exp(-jnp.abs(pre)))) * (1.0 / GLA_TAU)
    bf_ref[...] = _dot_hi(tril_ref[...], log_a)
    qf_ref[...] = q_ref[...].astype(F32) * (GLA_DK ** -0.5)
    kf_ref[...] = k_ref[...].astype(F32)
    vf_ref[...] = v_ref[...].astype(F32)
    e_mat = e_ref[...]
    row_id = lax.broadcasted_iota(I32, (GLA_SUB, GLA_KW), 0)

    def chunk(ci, carry):
        r0 = pl.multiple_of(ci * GLA_C, GLA_C)
        q = qf_ref[pl.ds(r0, GLA_C), :]
        k = kf_ref[pl.ds(r0, GLA_C), :]
        v = vf_ref[pl.ds(r0, GLA_C), :]
        b = bf_ref[pl.ds(r0, GLA_C), :]
        st = st_ref[...]
        o_inter = _dot_nt((q * jnp.exp(b)).astype(BF16), st.astype(BF16))
        b_last = b[GLA_C - 1:GLA_C, :]
        kd = (k * jnp.exp(b_last - b)).astype(BF16)
        upd = _dot(v.T.astype(BF16), kd) * et_ref[...]
        st_ref[...] = st * jnp.exp(b_last) + upd
        outs = []
        for i in range(GLA_NSUB):
            qi = q[i * GLA_SUB:(i + 1) * GLA_SUB, :]
            bi = b[i * GLA_SUB:(i + 1) * GLA_SUB, :]
            acc = o_inter[i * GLA_SUB:(i + 1) * GLA_SUB, :]
            for j in range(i + 1):
                slabs = []
                for s in range(GLA_SUB):
                    row = r0 + (j * GLA_SUB + s)
                    ks = kf_ref[pl.ds(row, 1), :]
                    bs = bf_ref[pl.ds(row, 1), :]
                    p = qi * ks * jnp.exp(jnp.minimum(bi - bs, 0.0))
                    if i == j:
                        p = jnp.where(row_id >= s, p, 0.0)
                    slabs.append(p.astype(BF16))
                pr = _dot(jnp.concatenate(slabs, axis=0), e_mat)
                for s in range(GLA_SUB):
                    vs = vf_ref[pl.ds(r0 + (j * GLA_SUB + s), 1), :]
                    acc = acc + pr[s * GLA_SUB:(s + 1) * GLA_SUB, :] * vs
            outs.append(acc)
        of_ref[pl.ds(r0, GLA_C), :] = jnp.concatenate(outs, axis=0)
        return carry

    lax.fori_loop(0, rows // GLA_C, chunk, 0)

    o = of_ref[...]
    ss = _dot_hi(o * o, e2_ref[...]) * (1.0 / GLA_DV)
    o = o * lax.rsqrt(ss + NORM_EPS) * on_ref[...]
    r = r_ref[...].astype(F32)
    o_ref[...] = (o * (r / (1.0 + jnp.exp(-r)))).astype(BF16)


def _gla(z, wa2_p, ba_p, on_p, B, S, rows):
    T = z.shape[0]
    nblk = S // rows
    e = np.zeros((GLA_KW, GLA_VW), np.float32)
    for h in range(GLA_HEADS):
        e[h * GLA_DKP:(h + 1) * GLA_DKP, h * GLA_DV:(h + 1) * GLA_DV] = 1.0
    e2 = np.zeros((GLA_VW, GLA_VW), np.float32)
    for h in range(GLA_HEADS):
        e2[h * GLA_DV:(h + 1) * GLA_DV, h * GLA_DV:(h + 1) * GLA_DV] = 1.0
    tr = np.zeros((rows, rows), np.float32)
    for cc in range(rows // GLA_C):
        tr[cc * GLA_C:(cc + 1) * GLA_C, cc * GLA_C:(cc + 1) * GLA_C] = np.tril(np.ones((GLA_C, GLA_C)))
    rowmap = lambda b, c: (b * nblk + c)
    const = lambda b, c: (0, 0)
    return pl.pallas_call(
        functools.partial(_gla_kernel, rows=rows),
        out_shape=jax.ShapeDtypeStruct((T, GLA_VW), BF16),
        grid=(B, nblk),
        in_specs=[
            pl.BlockSpec((rows, GLA_KW), lambda b, c: (rowmap(b, c), OFF_GQ // GLA_KW)),
            pl.BlockSpec((rows, GLA_KW), lambda b, c: (rowmap(b, c), OFF_GK // GLA_KW)),
            pl.BlockSpec((rows, GLA_VW), lambda b, c: (rowmap(b, c), OFF_GV // GLA_VW)),
            pl.BlockSpec((rows, GLA_VW), lambda b, c: (rowmap(b, c), OFF_GR // GLA_VW)),
            pl.BlockSpec((rows, LANES), lambda b, c: (rowmap(b, c), OFF_LR // LANES)),
            pl.BlockSpec((LANES, GLA_KW), const),
            pl.BlockSpec((1, GLA_KW), const),
            pl.BlockSpec((1, GLA_VW), const),
            pl.BlockSpec((GLA_KW, GLA_VW), const),
            pl.BlockSpec((GLA_VW, GLA_VW), const),
            pl.BlockSpec((GLA_VW, GLA_KW), const),
            pl.BlockSpec((rows, rows), const),
        ],
        out_specs=pl.BlockSpec((rows, GLA_VW), lambda b, c: (rowmap(b, c), 0)),
        scratch_shapes=[
            pltpu.VMEM((GLA_VW, GLA_KW), F32),
            pltpu.VMEM((rows, GLA_KW), F32),
            pltpu.VMEM((rows, GLA_KW), F32),
            pltpu.VMEM((rows, GLA_VW), F32),
            pltpu.VMEM((rows, GLA_KW), F32),
            pltpu.VMEM((rows, GLA_VW), F32),
        ],
        compiler_params=_cparams(("arbitrary", "arbitrary")),
        name="gla",
    )(z, z, z, z, z, wa2_p, ba_p, on_p, jnp.asarray(e, BF16), jnp.asarray(e2), jnp.asarray(e.T),
      jnp.asarray(tr))


DSA_TQ = 256


def _t5_bucket(dist):
    max_exact = NUM_BUCKETS // 2
    d = jnp.maximum(dist, 1).astype(F32)
    large = max_exact + (jnp.log(d / max_exact) / math.log(MAX_DISTANCE / max_exact)
                         * (NUM_BUCKETS - max_exact)).astype(I32)
    large = jnp.minimum(large, NUM_BUCKETS - 1)
    return jnp.where(dist < max_exact, dist, large)


def _dsa_kernel(rb_ref, dq_ref, dk_ref, dv_ref, iq_ref, ik_ref, iw_ref, o_ref,
                bias_ref, key_ref, vt_ref, iqs_ref, lo_ref, hi_ref, clo_ref, chi_ref,
                done_ref, m_ref, acc_ref, *, seq, topk):
    TQ = DSA_TQ
    bi = pl.program_id(0)
    qb = pl.program_id(1)
    nkb = qb + 1
    log2_seq = int(math.log2(seq))

    s_row = lax.broadcasted_iota(I32, (TQ, TQ), 0)
    t_lane = lax.broadcasted_iota(I32, (TQ, TQ), 1)

    @pl.when((bi == 0) & (qb == 0))
    def _():
        for delta in range(3):
            bucket = _t5_bucket(jnp.maximum(delta * TQ + t_lane - s_row, 0))
            for hh in range(DSA_HEADS):
                tile = jnp.zeros((TQ, TQ), F32)
                for bk in range(NUM_BUCKETS):
                    tile = jnp.where(bucket == bk, rb_ref[bk, hh], tile)
                bias_ref[hh, delta] = tile

    @pl.when(qb == 0)
    def _():
        def tr(kb, carry):
            k0 = pl.multiple_of(kb * TQ, TQ)
            for hh in range(DSA_HEADS):
                blk = dv_ref[pl.ds(k0, TQ), hh * LANES:(hh + 1) * LANES].astype(F32)
                vt_ref[kb, hh] = blk.T.astype(BF16)
            return carry

        lax.fori_loop(0, seq // TQ, tr, 0)

    for hh in range(IDX_HEADS):
        iqs_ref[hh * TQ:(hh + 1) * TQ, :] = iq_ref[:, hh * IDX_DIM:(hh + 1) * IDX_DIM]
    w_t = iw_ref[...].astype(F32).T * ((IDX_HEADS ** -0.5) * (IDX_DIM ** -0.5))

    def score_block(kb, carry):
        k0 = pl.multiple_of(kb * TQ, TQ)
        ik = ik_ref[pl.ds(k0, TQ), :][:, :IDX_DIM]
        dots = _dot_nt(ik, iqs_ref[...])
        sc = jnp.zeros((TQ, TQ), F32)
        for hh in range(IDX_HEADS):
            sc = sc + (jnp.maximum(dots[:, hh * TQ:(hh + 1) * TQ], 0.0)
                       * w_t[IDX_DIM + hh:IDX_DIM + hh + 1, :])
        bits = pltpu.bitcast(sc, I32)
        keys = bits ^ ((bits >> 31) & INT_MAX)
        causal = (k0 + s_row) <= (qb * TQ + t_lane)
        key_ref[kb] = jnp.where(causal, keys, INT_MIN)
        return carry

    lax.fori_loop(0, nkb, score_block, 0)

    ncausal = qb * TQ + lax.broadcasted_iota(I32, (1, TQ), 1) + 1
    lo_ref[...] = jnp.full((1, TQ), INT_MIN + 1, I32)
    hi_ref[...] = jnp.full((1, TQ), INT_MAX, I32)
    clo_ref[...] = ncausal
    chi_ref[...] = jnp.zeros((1, TQ), I32)
    done_ref[...] = (ncausal <= topk).astype(I32)

    def count_cols(pred):
        def body(kb, cnt):
            hit = jnp.where(pred(key_ref[kb], kb), 1, 0)
            for r in range(TQ // 8):
                cnt = cnt + hit[r * 8:(r + 1) * 8, :]
            return cnt

        cnt = lax.fori_loop(0, nkb, body, jnp.zeros((8, TQ), I32))
        return jnp.sum(cnt, axis=0, keepdims=True)

    def bis_cond(state):
        it, active = state
        return (it < 34) & (active > 0)

    def bis_body(state):
        it, _ = state
        lo = lo_ref[...]
        hi = hi_ref[...]
        done = done_ref[...]
        mid = (lo >> 1) + (hi >> 1) + (lo & hi & 1)
        c = count_cols(lambda kk, kb: kk >= mid)
        live = done == 0
        conv = mid == lo
        ge = c >= topk
        take_lo = live & ge & jnp.logical_not(conv)
        take_hi = live & jnp.logical_not(ge)
        lo_ref[...] = jnp.where(take_lo, mid, lo)
        clo_ref[...] = jnp.where(take_lo, c, clo_ref[...])
        hi_ref[...] = jnp.where(take_hi, mid, hi)
        chi_ref[...] = jnp.where(take_hi, c, chi_ref[...])
        new_done = jnp.where(live & (conv | (c == topk)), 1, done)
        done_ref[...] = new_done
        return it + 1, jnp.sum(1 - new_done)

    lax.while_loop(bis_cond, bis_body, (jnp.int32(0), jnp.sum(1 - done_ref[...])))

    tie_q = (clo_ref[...] > topk) & (ncausal > topk)

    @pl.when(jnp.sum(tie_q.astype(I32)) > 0)
    def _():
        thr = lo_ref[...]
        need = topk - chi_ref[...]
        plo = jnp.full((1, TQ), -1, I32)
        phi = jnp.full((1, TQ), seq - 1, I32)
        for _ in range(log2_seq):
            pm = (plo + phi) >> 1
            c = count_cols(lambda kk, kb: (kk == thr) & ((kb * TQ + s_row) <= pm))
            ok = c >= need
            phi = jnp.where(ok, pm, phi)
            plo = jnp.where(ok, plo, pm)
        cut = jnp.where(tie_q, phi, seq - 1)

        def demote(kb, carry):
            kk = key_ref[kb]
            drop = (kk == thr) & ((kb * TQ + s_row) > cut)
            key_ref[kb] = jnp.where(drop, INT_MIN, kk)
            return carry

        lax.fori_loop(0, nkb, demote, 0)

    m_ref[...] = jnp.full(m_ref.shape, NEG, F32)
    acc_ref[...] = jnp.zeros_like(acc_ref)
    thr_sel = lo_ref[...]

    def attend(kb, carry):
        k0 = pl.multiple_of(kb * TQ, TQ)
        sel = key_ref[kb] >= thr_sel
        delta = jnp.minimum(qb - kb, 2)
        heads = range(DSA_HEADS)
        m_prev = [m_ref[hh] for hh in heads]
        acc_prev = [acc_ref[hh] for hh in heads]
        logits = [_dot_nt(dk_ref[pl.ds(k0, TQ), hh * LANES:(hh + 1) * LANES],
                          dq_ref[:, hh * LANES:(hh + 1) * LANES]) for hh in heads]
        logits = [jnp.where(sel, logits[hh] + bias_ref[hh, delta], NEG) for hh in heads]
        m_new = [jnp.maximum(m_prev[hh], jnp.max(logits[hh], axis=0, keepdims=True))
                 for hh in heads]
        p = [jnp.exp(logits[hh] - m_new[hh]).astype(BF16) for hh in heads]
        pv = [_dot(vt_ref[kb, hh], p[hh]) for hh in heads]
        for hh in heads:
            acc_ref[hh] = acc_prev[hh] * jnp.exp(m_prev[hh] - m_new[hh]) + pv[hh]
            m_ref[hh] = m_new[hh]
        return carry

    lax.fori_loop(0, nkb, attend, 0)

    outs = []
    for hh in range(DSA_HEADS):
        a = acc_ref[hh]
        outs.append(a[:DSA_DH, :] / a[DSA_DH:DSA_DH + 1, :])
    o_ref[...] = jnp.concatenate(outs, axis=0).T.astype(BF16)


def _dsa(z, rel_bias, B, S):
    T = z.shape[0]
    TQ = DSA_TQ
    nqb = S // TQ
    topk = min(TOPK_MAX, S // 4)
    qrow = lambda b, q: b * nqb + q
    return pl.pallas_call(
        functools.partial(_dsa_kernel, seq=S, topk=topk),
        out_shape=jax.ShapeDtypeStruct((T, DSA_HEADS * DSA_DH), BF16),
        grid=(B, nqb),
        in_specs=[
            pl.BlockSpec(memory_space=pltpu.SMEM),
            pl.BlockSpec((TQ, DSA_W), lambda b, q: (qrow(b, q), OFF_DQ // DSA_W)),
            pl.BlockSpec((S, DSA_W), lambda b, q: (b, OFF_DK // DSA_W)),
            pl.BlockSpec((S, DSA_W), lambda b, q: (b, OFF_DV // DSA_W)),
            pl.BlockSpec((TQ, IQ_W), lambda b, q: (qrow(b, q), OFF_IQ // IQ_W)),
            pl.BlockSpec((S, LANES), lambda b, q: (b, OFF_IK // LANES)),
            pl.BlockSpec((TQ, LANES), lambda b, q: (qrow(b, q), OFF_IK // LANES)),
        ],
        out_specs=pl.BlockSpec((TQ, DSA_HEADS * DSA_DH), lambda b, q: (qrow(b, q), 0)),
        scratch_shapes=[
            pltpu.VMEM((DSA_HEADS, 3, TQ, TQ), F32),
            pltpu.VMEM((S // TQ, TQ, TQ), I32),
            pltpu.VMEM((S // TQ, DSA_HEADS, LANES, TQ), BF16),
            pltpu.VMEM((IDX_HEADS * TQ, IDX_DIM), BF16),
            pltpu.VMEM((1, TQ), I32),
            pltpu.VMEM((1, TQ), I32),
            pltpu.VMEM((1, TQ), I32),
            pltpu.VMEM((1, TQ), I32),
            pltpu.VMEM((1, TQ), I32),
            pltpu.VMEM((DSA_HEADS, 1, TQ), F32),
            pltpu.VMEM((DSA_HEADS, LANES, TQ), F32),
        ],
        compiler_params=_cparams(("arbitrary", "arbitrary")),
        name="dsa",
    )(rel_bias, z, z, z, z, z, z)


CONV_HALO = 32


def _conv_kernel(u_ref, w_ref, cb_ref, g_ref, b_ref, o_ref, h_ref, *, tm):
    @pl.when(pl.program_id(1) == 0)
    def _():
        h_ref[0:CONV_HALO, :] = jnp.zeros((CONV_HALO, CONV_CH), F32)

    u = u_ref[...].astype(F32)
    a = u[:, :CONV_CH]
    g = u[:, CONV_CH:]
    h_ref[CONV_HALO:CONV_HALO + tm, :] = a / (1.0 + jnp.exp(-g))
    acc = jnp.zeros((tm, CONV_CH), F32) + cb_ref[...]
    base = CONV_HALO - (CONV_WIDTH - 1)
    for j in range(CONV_WIDTH):
        acc = acc + h_ref[base + j:base + j + tm, :] * w_ref[j:j + 1, :]
    h_ref[0:CONV_HALO, :] = h_ref[tm:tm + CONV_HALO, :]
    mu = jnp.mean(acc, axis=-1, keepdims=True)
    d = acc - mu
    var = jnp.mean(d * d, axis=-1, keepdims=True)
    y = d * lax.rsqrt(var + NORM_EPS) * g_ref[...] + b_ref[...]
    o_ref[...] = (y / (1.0 + jnp.exp(-y))).astype(BF16)


def _conv(z, w, cb, g, b, B, S, tm):
    T = z.shape[0]
    nblk = S // tm
    const = lambda bb, c: (0, 0)
    return pl.pallas_call(
        functools.partial(_conv_kernel, tm=tm),
        out_shape=jax.ShapeDtypeStruct((T, CONV_CH), BF16),
        grid=(B, nblk),
        in_specs=[
            pl.BlockSpec((tm, CU_W), lambda bb, c: (bb * nblk + c, OFF_CU // CU_W)),
            pl.BlockSpec((CONV_WIDTH, CONV_CH), const),
            pl.BlockSpec((1, CONV_CH), const),
            pl.BlockSpec((1, CONV_CH), const),
            pl.BlockSpec((1, CONV_CH), const),
        ],
        out_specs=pl.BlockSpec((tm, CONV_CH), lambda bb, c: (bb * nblk + c, 0)),
        scratch_shapes=[pltpu.VMEM((tm + CONV_HALO, CONV_CH), F32)],
        compiler_params=_cparams(("arbitrary", "arbitrary")),
        name="conv",
    )(z, w, cb, g, b)


XW = X_HEADS * X_DH


def _mem_kv_kernel(m_ref, g_ref, w_ref, kn_ref, e_ref, k_ref, v_ref, *, mlen):
    hm = _rms(m_ref[...], g_ref[...]).astype(BF16)
    kv = _dot(hm, w_ref[...])
    k = kv[:, :XW]
    v = kv[:, XW:]
    ss = _dot_hi(k * k, e_ref[...]) * (1.0 / X_DH)
    k = k * lax.rsqrt(ss + NORM_EPS) * kn_ref[...] * (X_DH ** -0.5)
    lane_head = lax.broadcasted_iota(I32, (mlen, XW), 1) // X_DH
    for hh in range(X_HEADS):
        msk = lane_head == hh
        k_ref[hh * mlen:(hh + 1) * mlen, :] = jnp.where(msk, k, 0.0).astype(BF16)
        v_ref[hh * mlen:(hh + 1) * mlen, :] = jnp.where(msk, v, 0.0).astype(BF16)


def _mem_kv(mem2d, g, w_ckv, kn_t, e_x, B, M):
    D = mem2d.shape[1]
    const = lambda b: (0, 0)
    out = jax.ShapeDtypeStruct((B * X_HEADS * M, XW), BF16)
    return pl.pallas_call(
        functools.partial(_mem_kv_kernel, mlen=M),
        out_shape=(out, out),
        grid=(B,),
        in_specs=[
            pl.BlockSpec((M, D), lambda b: (b, 0)),
            pl.BlockSpec((1, D), const),
            pl.BlockSpec((D, 2 * XW), const),
            pl.BlockSpec((1, XW), const),
            pl.BlockSpec((XW, XW), const),
        ],
        out_specs=(pl.BlockSpec((X_HEADS * M, XW), lambda b: (b, 0)),
                   pl.BlockSpec((X_HEADS * M, XW), lambda b: (b, 0))),
        compiler_params=_cparams(("arbitrary",)),
        name="mem_kv",
    )(mem2d, g, w_ckv, kn_t, e_x)


def _post_kernel(x_ref, og_ref, od_ref, oc_ref, wo_ref, nx_ref, wcq_ref, qn_ref, e_ref,
                 kbd_ref, vbd_ref, obd_ref, wco_ref, nf_ref, wr_ref, br_ref,
                 x2_ref, hf_ref, gate_ref, *, mlen):
    wo = wo_ref
    x1 = (x_ref[...]
          + _dot(og_ref[...], wo[0:GLA_VW, :])
          + _dot(od_ref[...], wo[GLA_VW:2 * GLA_VW, :])
          + _dot(oc_ref[...], wo[2 * GLA_VW:, :]))
    hx = _rms(x1, nx_ref[...]).astype(BF16)
    q = _dot(hx, wcq_ref[...])
    ss = _dot_hi(q * q, e_ref[...]) * (1.0 / X_DH)
    q = (q * lax.rsqrt(ss + NORM_EPS) * qn_ref[...]).astype(BF16)
    logits = _dot_nt(q, kbd_ref[...])
    ps = []
    for hh in range(X_HEADS):
        lg = logits[:, hh * mlen:(hh + 1) * mlen]
        ps.append(jnp.exp(lg - jnp.max(lg, axis=-1, keepdims=True)))
    p = jnp.concatenate(ps, axis=-1).astype(BF16)
    o = _dot(p, vbd_ref[...]) / _dot(p, obd_ref[...])
    x2 = x1 + _dot(o.astype(BF16), wco_ref[...])
    x2_ref[...] = x2
    hf = _rms(x2, nf_ref[...])
    hf_ref[...] = hf.astype(BF16)
    rl = _dot_hi(hf, wr_ref[...]) + br_ref[...]
    tm = rl.shape[0]
    lane = lax.broadcasted_iota(I32, (tm, LANES), 1)
    is_g = (lane >= N_EXPERTS) & (lane < N_EXPERTS + N_GROUPS)
    g_max = jnp.max(jnp.where(is_g, rl, NEG), axis=-1, keepdims=True)
    g_sel = jnp.min(jnp.where(is_g & (rl == g_max), lane, LANES), axis=-1, keepdims=True) - N_EXPERTS
    g_den = jnp.sum(jnp.where(is_g, jnp.exp(rl - g_max), 0.0), axis=-1, keepdims=True)
    g_w = 1.0 / g_den
    in_grp = (lane < N_EXPERTS) & ((lane // EXPERTS_PER_GROUP) == g_sel)
    e1 = jnp.max(jnp.where(in_grp, rl, NEG), axis=-1, keepdims=True)
    l1 = jnp.min(jnp.where(in_grp & (rl == e1), lane, LANES), axis=-1, keepdims=True)
    rest = in_grp & (lane != l1)
    e2 = jnp.max(jnp.where(rest, rl, NEG), axis=-1, keepdims=True)
    l2 = jnp.min(jnp.where(rest & (rl == e2), lane, LANES), axis=-1, keepdims=True)
    p2 = jnp.exp(e2 - e1)
    w1 = 1.0 / (1.0 + p2)
    w2 = p2 / (1.0 + p2)
    gate_ref[...] = jnp.where(lane == l1, w1 * g_w, jnp.where(lane == l2, w2 * g_w, 0.0))


def _post(x2d, o_gla, o_dsa, o_conv, w_out, nx, w_cq, qn_t, e_x, kbd, vbd, obd, w_co, nf, w_r, b_r,
          B, S, M, tm):
    T, D = x2d.shape
    nblk = S // tm
    const = lambda i: (0, 0)
    row = lambda i: (i, 0)
    bat = lambda i: (i // nblk, 0)
    HM = X_HEADS * M
    return pl.pallas_call(
        functools.partial(_post_kernel, mlen=M),
        out_shape=(jax.ShapeDtypeStruct((T, D), F32), jax.ShapeDtypeStruct((T, D), BF16),
                   jax.ShapeDtypeStruct((T, LANES), F32)),
        grid=(T // tm,),
        in_specs=[
            pl.BlockSpec((tm, D), row),
            pl.BlockSpec((tm, GLA_VW), row),
            pl.BlockSpec((tm, DSA_HEADS * DSA_DH), row),
            pl.BlockSpec((tm, CONV_CH), row),
            pl.BlockSpec((D, D), const),
            pl.BlockSpec((1, D), const),
            pl.BlockSpec((D, XW), const),
            pl.BlockSpec((1, XW), const),
            pl.BlockSpec((XW, XW), const),
            pl.BlockSpec((HM, XW), bat),
            pl.BlockSpec((HM, XW), bat),
            pl.BlockSpec((HM, XW), const),
            pl.BlockSpec((XW, D), const),
            pl.BlockSpec((1, D), const),
            pl.BlockSpec((D, LANES), const),
            pl.BlockSpec((1, LANES), const),
        ],
        out_specs=(pl.BlockSpec((tm, D), row), pl.BlockSpec((tm, D), row),
                   pl.BlockSpec((tm, LANES), row)),
        compiler_params=_cparams(("arbitrary",)),
        name="post",
    )(x2d, o_gla, o_dsa, o_conv, w_out, nx, w_cq, qn_t, e_x, kbd, vbd, obd, w_co, nf, w_r, b_r)


def _moe_kernel(x2_ref, hf_ref, gate_ref, wg_ref, wu_ref, wd_ref, o_ref):
    e = pl.program_id(1)

    @pl.when(e == 0)
    def _():
        o_ref[...] = x2_ref[...]

    t = hf_ref[...]
    lane = lax.broadcasted_iota(I32, gate_ref.shape, 1)
    g = jnp.sum(jnp.where(lane == e, gate_ref[...], 0.0), axis=-1, keepdims=True)
    a = _dot(t, wg_ref[0])
    u = _dot(t, wu_ref[0])
    he = (a / (1.0 + jnp.exp(-a))) * u * g
    o_ref[...] += _dot(he.astype(BF16), wd_ref[0])


def _moe(x2, hf, gates, wg, wu, wd, tm):
    T, D = x2.shape
    DE = wg.shape[-1]
    row = lambda i, e: (i, 0)
    return pl.pallas_call(
        _moe_kernel,
        out_shape=jax.ShapeDtypeStruct((T, D), F32),
        grid=(T // tm, N_EXPERTS),
        in_specs=[
            pl.BlockSpec((tm, D), row),
            pl.BlockSpec((tm, D), row),
            pl.BlockSpec((tm, LANES), row),
            pl.BlockSpec((1, D, DE), lambda i, e: (e, 0, 0)),
            pl.BlockSpec((1, D, DE), lambda i, e: (e, 0, 0)),
            pl.BlockSpec((1, DE, D), lambda i, e: (e, 0, 0)),
        ],
        out_specs=pl.BlockSpec((tm, D), row),
        compiler_params=_cparams(("arbitrary", "arbitrary")),
        name="moe",
    )(x2, hf, gates, wg, wu, wd)


def _pad_heads(w, heads, dh, slot):
    lead = w.shape[:-1]
    w = w.reshape(lead + (heads, dh))
    w = jnp.pad(w, [(0, 0)] * len(lead) + [(0, 0), (0, slot - dh)])
    return w.reshape(lead + (heads * slot,))


def _pad_cols(w, width):
    return jnp.pad(w, [(0, 0)] * (w.ndim - 1) + [(0, width - w.shape[-1])])


def _layout_w_in(w):
    sizes = (GLA_HEADS * GLA_DK, GLA_HEADS * GLA_DK, GLA_VW, GLA_RANK, GLA_VW,
             DSA_HEADS * DSA_DH, DSA_HEADS * DSA_DH, DSA_HEADS * DSA_DH, IQ_W, IDX_DIM, IDX_HEADS,
             CU_W)
    offs = np.cumsum((0,) + sizes)
    g_q, g_k, g_v, g_lr, g_r, d_q, d_k, d_v, i_q, i_k, i_w, c_u = [
        w[:, offs[n]:offs[n + 1]] for n in range(len(sizes))]
    cols = [
        _pad_heads(d_q, DSA_HEADS, DSA_DH, LANES),
        _pad_heads(d_k, DSA_HEADS, DSA_DH, LANES),
        _pad_heads(d_v, DSA_HEADS, DSA_DH, LANES),
        g_v, g_r,
        _pad_heads(g_q, GLA_HEADS, GLA_DK, GLA_DKP),
        _pad_heads(g_k, GLA_HEADS, GLA_DK, GLA_DKP),
        i_q, c_u,
        _pad_cols(g_lr, LANES),
        _pad_cols(jnp.concatenate([i_k, i_w], axis=-1), LANES),
    ]
    out = jnp.concatenate(cols, axis=-1)
    assert out.shape[-1] == NP_IN
    return out.astype(BF16)


def _block_ones(n, blk):
    idx = np.arange(n) // blk
    return jnp.asarray((idx[:, None] == idx[None, :]).astype(np.float32))


def _layer(x2d, mem2d, rel_bias, p, B, S, M):
    D = x2d.shape[1]
    row = lambda v: v.reshape(1, -1)
    ones_dv = np.zeros((1, DSA_W), np.float32)
    ones_dv[0, DSA_DH::LANES] = 1.0
    z = _in_proj(x2d, row(p["norm_mix"]), _layout_w_in(p["w_in"]),
                 _pad_cols(row(p["dsa_qnorm"]), LANES), _pad_cols(row(p["dsa_knorm"]), LANES),
                 jnp.asarray(ones_dv), tm=min(512, S))
    wa2_p = jnp.pad(_pad_heads(p["gla_wa2"], GLA_HEADS, GLA_DK, GLA_DKP),
                    ((0, LANES - GLA_RANK), (0, 0)))
    o_gla = _gla(z, wa2_p, row(_pad_heads(p["gla_ba"], GLA_HEADS, GLA_DK, GLA_DKP)),
                 row(jnp.tile(p["gla_onorm"], GLA_HEADS)), B, S, rows=min(256, S))
    o_dsa = _dsa(z, rel_bias, B, S)
    o_conv = _conv(z, p["conv_w"], row(p["conv_b"]), row(p["conv_ln_g"]), row(p["conv_ln_b"]),
                   B, S, tm=min(512, S))
    e_x = _block_ones(XW, X_DH)
    kbd, vbd = _mem_kv(mem2d, row(p["norm_mem"]), p["w_ckv"].astype(BF16),
                       row(jnp.tile(p["x_knorm"], X_HEADS)), e_x, B, M)
    obd = (jnp.arange(X_HEADS * M)[:, None] // M == jnp.arange(XW)[None, :] // X_DH).astype(BF16)
    w_r = _pad_cols(jnp.concatenate([p["w_re"], p["w_rg"]], axis=-1), LANES)
    b_r = _pad_cols(row(jnp.concatenate([p["b_re"], p["b_rg"]])), LANES)
    x2, hf, gates = _post(
        x2d, o_gla, o_dsa, o_conv, p["w_out"].astype(BF16), row(p["norm_x"]),
        p["w_cq"].astype(BF16), row(jnp.tile(p["x_qnorm"], X_HEADS)), e_x, kbd, vbd, obd,
        p["w_co"].astype(BF16), row(p["norm_ffn"]), w_r, b_r, B, S, M, tm=min(256, S))
    return _moe(x2, hf, gates, p["w_gate"].astype(BF16), p["w_up"].astype(BF16),
                p["w_down"].astype(BF16), tm=min(1024, S))


_LAYER_KEYS = ("norm_mix", "w_in", "gla_wa2", "gla_ba", "gla_onorm", "dsa_qnorm", "dsa_knorm",
               "conv_w", "conv_b", "conv_ln_g", "conv_ln_b", "w_out", "norm_x", "norm_mem", "w_cq",
               "w_ckv", "x_qnorm", "x_knorm", "w_co", "norm_ffn", "w_rg", "b_rg", "w_re", "b_re",
               "w_gate", "w_up", "w_down")


def kernel(x, mem, rel_bias, norm_mix, w_in, gla_wa2, gla_ba, gla_onorm, dsa_qnorm, dsa_knorm,
           conv_w, conv_b, conv_ln_g, conv_ln_b, w_out, norm_x, norm_mem, w_cq, w_ckv, x_qnorm,
           x_knorm, w_co, norm_ffn, w_rg, b_rg, w_re, b_re, w_gate, w_up, w_down):
    stacked = dict(zip(_LAYER_KEYS, (
        norm_mix, w_in, gla_wa2, gla_ba, gla_onorm, dsa_qnorm, dsa_knorm, conv_w, conv_b,
        conv_ln_g, conv_ln_b, w_out, norm_x, norm_mem, w_cq, w_ckv, x_qnorm, x_knorm, w_co,
        norm_ffn, w_rg, b_rg, w_re, b_re, w_gate, w_up, w_down)))
    B, S, D = x.shape
    M = mem.shape[1]
    x2d = x.reshape(B * S, D)
    mem2d = mem.reshape(B * M, D)
    for l in range(norm_mix.shape[0]):
        x2d = _layer(x2d, mem2d, rel_bias, {k: v[l] for k, v in stacked.items()}, B, S, M)
    return x2d.reshape(B, S, D)
```

```python
import functools
import math

import jax
import jax.numpy as jnp
import numpy as np
from jax import lax
from jax.experimental import pallas as pl
from jax.experimental.pallas import tpu as pltpu

F32 = jnp.float32
BF16 = jnp.bfloat16
I32 = jnp.int32

LANES = 128
NORM_EPS = 1e-6

GLA_HEADS = 4
GLA_DK = 48
GLA_DV = 96
GLA_DKP = 64
GLA_RANK = 16
GLA_TAU = 16.0
DSA_HEADS = 6
DSA_DH = 64
IDX_HEADS = 8
IDX_DIM = 64
TOPK_MAX = 256
CONV_CH = 256
CONV_WIDTH = 31
NUM_BUCKETS = 32
MAX_DISTANCE = 128
X_HEADS = 4
X_DH = 64
N_GROUPS = 4
EXPERTS_PER_GROUP = 4
N_EXPERTS = 16

DSA_W = DSA_HEADS * LANES
GLA_VW = GLA_HEADS * GLA_DV
GLA_KW = GLA_HEADS * GLA_DKP
IQ_W = IDX_HEADS * IDX_DIM
CU_W = 2 * CONV_CH
OFF_DQ, OFF_DK, OFF_DV = 0, DSA_W, 2 * DSA_W
OFF_GV = 3 * DSA_W
OFF_GR = OFF_GV + GLA_VW
OFF_GQ = OFF_GR + GLA_VW
OFF_GK = OFF_GQ + GLA_KW
OFF_IQ = OFF_GK + GLA_KW
OFF_CU = OFF_IQ + IQ_W
OFF_LR = OFF_CU + CU_W
OFF_IK = OFF_LR + LANES
NP_IN = OFF_IK + LANES

NEG = -1e30
LOG2E = math.log2(math.e)
INT_MIN = -2 ** 31
INT_MAX = 2 ** 31 - 1

VMEM_LIMIT = 56 * 1024 * 1024


def _cparams(sem):
    return pltpu.CompilerParams(dimension_semantics=sem, vmem_limit_bytes=VMEM_LIMIT)


def _dot(a, b):
    return jnp.dot(a, b, preferred_element_type=F32)


def _dot_nt(a, b):
    return lax.dot_general(a, b, (((1,), (1,)), ((), ())), preferred_element_type=F32)


def _dot_hi(a, b):
    return jnp.dot(a, b, preferred_element_type=F32, precision=lax.Precision.HIGHEST)


def _rms(x, g):
    return x * lax.rsqrt(jnp.mean(x * x, axis=-1, keepdims=True) + NORM_EPS) * g


_IN_CHUNKS = ((OFF_DQ, DSA_W), (OFF_DK, DSA_W), (OFF_DV, DSA_W), (OFF_GV, 2 * GLA_VW),
              (OFF_GQ, 2 * GLA_KW), (OFF_IQ, IQ_W), (OFF_CU, CU_W), (OFF_LR, 2 * LANES))


def _in_proj_kernel(x_ref, g_ref, w_ref, qn_ref, kn_ref, ones_ref, o_ref):
    h = _rms(x_ref[...], g_ref[...]).astype(BF16)

    def head_norm(y, gain, scale):
        parts = []
        for hh in range(DSA_HEADS):
            yh = y[:, hh * LANES:(hh + 1) * LANES]
            ss = jnp.sum(yh * yh, axis=-1, keepdims=True) * (1.0 / DSA_DH)
            parts.append(yh * lax.rsqrt(ss + NORM_EPS) * gain * scale)
        return jnp.concatenate(parts, axis=-1)

    for c0, cw in _IN_CHUNKS:
        y = _dot(h, w_ref[:, c0:c0 + cw])
        if c0 == OFF_DQ:
            y = head_norm(y, qn_ref[...], DSA_DH ** -0.5 * LOG2E)
        elif c0 == OFF_DK:
            y = head_norm(y, kn_ref[...], 1.0)
        elif c0 == OFF_DV:
            y = y + ones_ref[...]
        o_ref[:, c0:c0 + cw] = y.astype(BF16)


def _in_proj(x2d, g, w_p, qn_p, kn_p, ones_p, tm):
    T, D = x2d.shape
    return pl.pallas_call(
        _in_proj_kernel,
        out_shape=jax.ShapeDtypeStruct((T, NP_IN), BF16),
        grid=(T // tm,),
        in_specs=[
            pl.BlockSpec((tm, D), lambda i: (i, 0)),
            pl.BlockSpec((1, D), lambda i: (0, 0)),
            pl.BlockSpec((D, NP_IN), lambda i: (0, 0)),
            pl.BlockSpec((1, LANES), lambda i: (0, 0)),
            pl.BlockSpec((1, LANES), lambda i: (0, 0)),
            pl.BlockSpec((1, DSA_W), lambda i: (0, 0)),
        ],
        out_specs=pl.BlockSpec((tm, NP_IN), lambda i: (i, 0)),
        compiler_params=_cparams(("arbitrary",)),
        name="in_proj",
    )(x2d, g, w_p, qn_p, kn_p, ones_p)


GLA_C = 64
GLA_SUB = 16
GLA_NSUB = GLA_C // GLA_SUB


def _gla_kernel(q_ref, k_ref, v_ref, r_ref, lr_ref, wa2_ref, ba_ref, on_ref, e_ref, e2_ref,
                et_ref, tril_ref, o_ref, st_ref, qf_ref, kf_ref, vf_ref, bf_ref, of_ref, *, rows):
    c = pl.program_id(1)

    @pl.when(c == 0)
    def _():
        st_ref[...] = jnp.zeros_like(st_ref)

    pre = _dot_hi(lr_ref[...].astype(F32), wa2_ref[...]) + ba_ref[...]
    log_a = (jnp.minimum(pre, 0.0) - jnp.log(1.0 + jnp.exp(-jnp.abs(pre)))) * (1.0 / GLA_TAU)
    bf_ref[...] = _dot_hi(tril_ref[...], log_a)
    qf_ref[...] = q_ref[...].astype(F32) * (GLA_DK ** -0.5)
    kf_ref[...] = k_ref[...].astype(F32)
    vf_ref[...] = v_ref[...].astype(F32)
    e_mat = e_ref[...]
    row_id = lax.broadcasted_iota(I32, (GLA_SUB, GLA_KW), 0)

    def chunk(ci, carry):
        r0 = pl.multiple_of(ci * GLA_C, GLA_C)
        q = qf_ref[pl.ds(r0, GLA_C), :]
        k = kf_ref[pl.ds(r0, GLA_C), :]
        v = vf_ref[pl.ds(r0, GLA_C), :]
        b = bf_ref[pl.ds(r0, GLA_C), :]
        st = st_ref[...]
        o_inter = _dot_nt((q * jnp.exp(b)).astype(BF16), st.astype(BF16))
        b_last = b[GLA_C - 1:GLA_C, :]
        kd = (k * jnp.exp(b_last - b)).astype(BF16)
        upd = _dot(v.T.astype(BF16), kd) * et_ref[...]
        st_ref[...] = st * jnp.exp(b_last) + upd
        outs = []
        for i in range(GLA_NSUB):
            qi = q[i * GLA_SUB:(i + 1) * GLA_SUB, :]
            bi = b[i * GLA_SUB:(i + 1) * GLA_SUB, :]
            acc = o_inter[i * GLA_SUB:(i + 1) * GLA_SUB, :]
            for j in range(i + 1):
                slabs = []
                for s in range(GLA_SUB):
                    row = r0 + (j * GLA_SUB + s)
                    ks = kf_ref[pl.ds(row, 1), :]
                    bs = bf_ref[pl.ds(row, 1), :]
                    p = qi * ks * jnp.exp(jnp.minimum(bi - bs, 0.0))
                    if i == j:
                        p = jnp.where(row_id >= s, p, 0.0)
                    slabs.append(p.astype(BF16))
                pr = _dot(jnp.concatenate(slabs, axis=0), e_mat)
                for s in range(GLA_SUB):
                    vs = vf_ref[pl.ds(r0 + (j * GLA_SUB + s), 1), :]
                    acc = acc + pr[s * GLA_SUB:(s + 1) * GLA_SUB, :] * vs
            outs.append(acc)
        of_ref[pl.ds(r0, GLA_C), :] = jnp.concatenate(outs, axis=0)
        return carry

    lax.fori_loop(0, rows // GLA_C, chunk, 0)

    o = of_ref[...]
    ss = _dot_hi(o * o, e2_ref[...]) * (1.0 / GLA_DV)
    o = o * lax.rsqrt(ss + NORM_EPS) * on_ref[...]
    r = r_ref[...].astype(F32)
    o_ref[...] = (o * (r / (1.0 + jnp.exp(-r)))).astype(BF16)


def _gla(z, wa2_p, ba_p, on_p, B, S, rows):
    T = z.shape[0]
    nblk = S // rows
    e = np.zeros((GLA_KW, GLA_VW), np.float32)
    for h in range(GLA_HEADS):
        e[h * GLA_DKP:(h + 1) * GLA_DKP, h * GLA_DV:(h + 1) * GLA_DV] = 1.0
    e2 = np.zeros((GLA_VW, GLA_VW), np.float32)
    for h in range(GLA_HEADS):
        e2[h * GLA_DV:(h + 1) * GLA_DV, h * GLA_DV:(h + 1) * GLA_DV] = 1.0
    tr = np.zeros((rows, rows), np.float32)
    for cc in range(rows // GLA_C):
        tr[cc * GLA_C:(cc + 1) * GLA_C, cc * GLA_C:(cc + 1) * GLA_C] = np.tril(np.ones((GLA_C, GLA_C)))
    rowmap = lambda b, c: (b * nblk + c)
    const = lambda b, c: (0, 0)
    return pl.pallas_call(
        functools.partial(_gla_kernel, rows=rows),
        out_shape=jax.ShapeDtypeStruct((T, GLA_VW), BF16),
        grid=(B, nblk),
        in_specs=[
            pl.BlockSpec((rows, GLA_KW), lambda b, c: (rowmap(b, c), OFF_GQ // GLA_KW)),
            pl.BlockSpec((rows, GLA_KW), lambda b, c: (rowmap(b, c), OFF_GK // GLA_KW)),
            pl.BlockSpec((rows, GLA_VW), lambda b, c: (rowmap(b, c), OFF_GV // GLA_VW)),
            pl.BlockSpec((rows, GLA_VW), lambda b, c: (rowmap(b, c), OFF_GR // GLA_VW)),
            pl.BlockSpec((rows, LANES), lambda b, c: (rowmap(b, c), OFF_LR // LANES)),
            pl.BlockSpec((LANES, GLA_KW), const),
            pl.BlockSpec((1, GLA_KW), const),
            pl.BlockSpec((1, GLA_VW), const),
            pl.BlockSpec((GLA_KW, GLA_VW), const),
            pl.BlockSpec((GLA_VW, GLA_VW), const),
            pl.BlockSpec((GLA_VW, GLA_KW), const),
            pl.BlockSpec((rows, rows), const),
        ],
        out_specs=pl.BlockSpec((rows, GLA_VW), lambda b, c: (rowmap(b, c), 0)),
        scratch_shapes=[
            pltpu.VMEM((GLA_VW, GLA_KW), F32),
            pltpu.VMEM((rows, GLA_KW), F32),
            pltpu.VMEM((rows, GLA_KW), F32),
            pltpu.VMEM((rows, GLA_VW), F32),
            pltpu.VMEM((rows, GLA_KW), F32),
            pltpu.VMEM((rows, GLA_VW), F32),
        ],
        compiler_params=_cparams(("arbitrary", "arbitrary")),
        name="gla",
    )(z, z, z, z, z, wa2_p, ba_p, on_p, jnp.asarray(e, BF16), jnp.asarray(e2), jnp.asarray(e.T),
      jnp.asarray(tr))


DSA_TQ = 256


def _t5_bucket(dist):
    max_exact = NUM_BUCKETS // 2
    d = jnp.maximum(dist, 1).astype(F32)
    large = max_exact + (jnp.log(d / max_exact) / math.log(MAX_DISTANCE / max_exact)
                         * (NUM_BUCKETS - max_exact)).astype(I32)
    large = jnp.minimum(large, NUM_BUCKETS - 1)
    return jnp.where(dist < max_exact, dist, large)


def _block_loop(start, stop, body, init):
    npair = (stop - start) // 2
    carry = lax.fori_loop(0, npair, lambda i, c: body([start + 2 * i, start + 2 * i + 1], c), init)
    return lax.cond((stop - start) % 2 == 1, lambda c: body([stop - 1], c), lambda c: c, carry)


def _dsa_kernel(rb_ref, dq_ref, dk_ref, dv_ref, iq_ref, ik_ref, iw_ref, o_ref,
                bias_ref, key_ref, key16_ref, vt_ref, iqs_ref, lo_ref, hi_ref, clo_ref, chi_ref,
                m_ref, acc_ref, lga_ref, lgb_ref, *, seq, topk):
    TQ = DSA_TQ
    bi = pl.program_id(0)
    qb = pl.program_id(1)
    nkb = qb + 1
    log2_seq = int(math.log2(seq))

    s_row = lax.broadcasted_iota(I32, (TQ, TQ), 0)
    t_lane = lax.broadcasted_iota(I32, (TQ, TQ), 1)

    @pl.when((bi == 0) & (qb == 0))
    def _():
        for delta in range(2):
            bucket = _t5_bucket(jnp.maximum(delta * TQ + t_lane - s_row, 0))
            for hh in range(DSA_HEADS):
                tile = jnp.zeros((TQ, TQ), F32)
                for bk in range(NUM_BUCKETS):
                    tile = jnp.where(bucket == bk, rb_ref[bk, hh] * LOG2E, tile)
                bias_ref[hh, delta] = tile

    @pl.when(qb == 0)
    def _():
        def tr(kb, carry):
            k0 = pl.multiple_of(kb * TQ, TQ)
            for hh in range(DSA_HEADS):
                blk = dv_ref[pl.ds(k0, TQ), hh * LANES:(hh + 1) * LANES].astype(F32)
                vt_ref[kb, hh] = blk.T.astype(BF16)
            return carry

        lax.fori_loop(0, seq // TQ, tr, 0)

    for hh in range(IDX_HEADS):
        iqs_ref[hh * TQ:(hh + 1) * TQ, :] = iq_ref[:, hh * IDX_DIM:(hh + 1) * IDX_DIM]
    w_t = iw_ref[...].astype(F32).T * ((IDX_HEADS ** -0.5) * (IDX_DIM ** -0.5))

    def score_blocks(kbs, carry):
        k0s = [pl.multiple_of(kb * TQ, TQ) for kb in kbs]
        dots = [_dot_nt(ik_ref[pl.ds(k0, TQ), :][:, :IDX_DIM], iqs_ref[...]) for k0 in k0s]
        for kb, k0, dt in zip(kbs, k0s, dots):
            sc = jnp.zeros((TQ, TQ), F32)
            for hh in range(IDX_HEADS):
                sc = sc + (jnp.maximum(dt[:, hh * TQ:(hh + 1) * TQ], 0.0)
                           * w_t[IDX_DIM + hh:IDX_DIM + hh + 1, :])
            bits = pltpu.bitcast(sc, I32)
            keys = bits ^ ((bits >> 31) & INT_MAX)
            causal = (k0 + s_row) <= (qb * TQ + t_lane)
            keys = jnp.where(causal, keys, INT_MIN)
            key_ref[kb] = keys
            key16_ref[kb] = (keys >> 16).astype(jnp.int16)
        return carry

    _block_loop(0, nkb, score_blocks, 0)

    ncausal = qb * TQ + lax.broadcasted_iota(I32, (1, TQ), 1) + 1
    short = ncausal <= topk

    def count_cols(pred, dtype, ref):
        rows = 8 * (4 // jnp.dtype(dtype).itemsize)

        def body(kbs, cnt):
            for kb in kbs:
                hit = jnp.where(pred(ref[kb], kb), jnp.ones((), dtype), jnp.zeros((), dtype))
                for r in range(TQ // rows):
                    cnt = cnt + hit[r * rows:(r + 1) * rows, :]
            return cnt

        cnt = _block_loop(0, nkb, body, jnp.zeros((rows, TQ), dtype))
        return jnp.sum(cnt.astype(I32), axis=0, keepdims=True)

    def bisect_step(state, count_ge):
        lo, hi, clo, chi = state
        mid = (lo >> 1) + (hi >> 1) + (lo & hi & 1)
        c = count_ge(mid)
        up = (c >= topk) & (mid != lo)
        down = c < topk
        return (jnp.where(up, mid, lo), jnp.where(down, mid, hi),
                jnp.where(up, c, clo), jnp.where(down, c, chi))

    def step16(_, state):
        return bisect_step(state, lambda mid: count_cols(
            lambda kk, kb: kk >= mid.astype(jnp.int16), jnp.int16, key16_ref))

    lo16, _, clo, chi = lax.fori_loop(
        0, 16, step16,
        (jnp.full((1, TQ), -2 ** 15 + 1, I32), jnp.full((1, TQ), 2 ** 15, I32),
         ncausal, jnp.zeros((1, TQ), I32)))
    lo_ref[...] = jnp.where(short, INT_MIN + 1, lo16 << 16)
    hi_ref[...] = jnp.where(lo16 == 2 ** 15 - 1, INT_MAX, (lo16 + 1) << 16)
    clo_ref[...] = clo
    chi_ref[...] = chi

    def active_count():
        lo = lo_ref[...]
        open_q = (clo_ref[...] != topk) & (hi_ref[...] - lo > 1) & jnp.logical_not(short)
        return jnp.sum(open_q.astype(I32))

    def bis_body(state):
        it, _ = state
        st = (lo_ref[...], hi_ref[...], clo_ref[...], chi_ref[...])
        for _ in range(2):
            frozen = st[2] == topk
            new = bisect_step(st, lambda mid: count_cols(lambda kk, kb: kk >= mid, I32, key_ref))
            st = tuple(jnp.where(frozen, o, n) for o, n in zip(st, new))
        lo_ref[...], hi_ref[...], clo_ref[...], chi_ref[...] = st
        return it + 1, active_count()

    lax.while_loop(lambda s_: (s_[0] < 9) & (s_[1] > 0), bis_body, (jnp.int32(0), active_count()))

    tie_q = (clo_ref[...] > topk) & jnp.logical_not(short)

    @pl.when(jnp.sum(tie_q.astype(I32)) > 0)
    def _():
        thr = lo_ref[...]
        need = topk - chi_ref[...]
        plo = jnp.full((1, TQ), -1, I32)
        phi = jnp.full((1, TQ), seq - 1, I32)
        for _ in range(log2_seq):
            pm = (plo + phi) >> 1
            c = count_cols(lambda kk, kb: (kk == thr) & ((kb * TQ + s_row) <= pm), I32, key_ref)
            ok = c >= need
            phi = jnp.where(ok, pm, phi)
            plo = jnp.where(ok, plo, pm)
        cut = jnp.where(tie_q, phi, seq - 1)

        def demote(kb, carry):
            kk = key_ref[kb]
            drop = (kk == thr) & ((kb * TQ + s_row) > cut)
            key_ref[kb] = jnp.where(drop, INT_MIN, kk)
            return carry

        lax.fori_loop(0, nkb, demote, 0)

    m_ref[...] = jnp.full(m_ref.shape, NEG, F32)
    acc_ref[...] = jnp.zeros_like(acc_ref)
    thr_sel = lo_ref[...]

    heads = range(DSA_HEADS)

    def qk(kb, hh):
        k0 = pl.multiple_of(kb * TQ, TQ)
        return _dot_nt(dk_ref[pl.ds(k0, TQ), hh * LANES:(hh + 1) * LANES],
                       dq_ref[:, hh * LANES:(hh + 1) * LANES])

    def softmax_update(kb, logits, shift):
        sel = key_ref[kb] >= thr_sel
        m_prev = [m_ref[hh] for hh in heads]
        acc_prev = [acc_ref[hh] for hh in heads]
        lg = [jnp.where(sel, logits[hh], NEG) for hh in heads]
        m_new = [jnp.maximum(m_prev[hh], jnp.max(lg[hh], axis=0, keepdims=True) + shift[hh])
                 for hh in heads]
        p = [jnp.exp2(lg[hh] - (m_new[hh] - shift[hh])).astype(BF16) for hh in heads]
        pv = [_dot(vt_ref[kb, hh], p[hh]) for hh in heads]
        for hh in heads:
            acc_ref[hh] = acc_prev[hh] * jnp.exp2(m_prev[hh] - m_new[hh]) + pv[hh]
            m_ref[hh] = m_new[hh]

    n_far = jnp.maximum(qb - 1, 0)
    far_shift = [rb_ref[NUM_BUCKETS - 1, hh] * LOG2E for hh in heads]

    @pl.when(n_far > 0)
    def _():
        for hh in heads:
            lga_ref[hh] = qk(0, hh)

    def far_pair(i, carry):
        ka = 2 * i
        for hh in heads:
            lgb_ref[hh] = qk(ka + 1, hh)
        softmax_update(ka, [lga_ref[hh] for hh in heads], far_shift)
        for hh in heads:
            lga_ref[hh] = qk(jnp.minimum(ka + 2, n_far - 1), hh)
        softmax_update(ka + 1, [lgb_ref[hh] for hh in heads], far_shift)
        return carry

    lax.fori_loop(0, n_far // 2, far_pair, 0)

    @pl.when(n_far % 2 == 1)
    def _():
        softmax_update(n_far - 1, [lga_ref[hh] for hh in heads], far_shift)

    def near_body(kb, carry):
        softmax_update(kb, [qk(kb, hh) + bias_ref[hh, qb - kb] for hh in heads], [0.0] * DSA_HEADS)
        return carry

    lax.fori_loop(n_far, nkb, near_body, 0)

    outs = []
    for hh in range(DSA_HEADS):
        a = acc_ref[hh]
        outs.append(a[:DSA_DH, :] / a[DSA_DH:DSA_DH + 1, :])
    o_ref[...] = jnp.concatenate(outs, axis=0).T.astype(BF16)


def _dsa(z, rel_bias, B, S):
    T = z.shape[0]
    TQ = DSA_TQ
    nqb = S // TQ
    topk = min(TOPK_MAX, S // 4)
    qrow = lambda b, q: b * nqb + q
    return pl.pallas_call(
        functools.partial(_dsa_kernel, seq=S, topk=topk),
        out_shape=jax.ShapeDtypeStruct((T, DSA_HEADS * DSA_DH), BF16),
        grid=(B, nqb),
        in_specs=[
            pl.BlockSpec(memory_space=pltpu.SMEM),
            pl.BlockSpec((TQ, DSA_W), lambda b, q: (qrow(b, q), OFF_DQ // DSA_W)),
            pl.BlockSpec((S, DSA_W), lambda b, q: (b, OFF_DK // DSA_W)),
            pl.BlockSpec((S, DSA_W), lambda b, q: (b, OFF_DV // DSA_W)),
            pl.BlockSpec((TQ, IQ_W), lambda b, q: (qrow(b, q), OFF_IQ // IQ_W)),
            pl.BlockSpec((S, LANES), lambda b, q: (b, OFF_IK // LANES)),
            pl.BlockSpec((TQ, LANES), lambda b, q: (qrow(b, q), OFF_IK // LANES)),
        ],
        out_specs=pl.BlockSpec((TQ, DSA_HEADS * DSA_DH), lambda b, q: (qrow(b, q), 0)),
        scratch_shapes=[
            pltpu.VMEM((DSA_HEADS, 2, TQ, TQ), F32),
            pltpu.VMEM((S // TQ, TQ, TQ), I32),
            pltpu.VMEM((S // TQ, TQ, TQ), jnp.int16),
            pltpu.VMEM((S // TQ, DSA_HEADS, LANES, TQ), BF16),
            pltpu.VMEM((IDX_HEADS * TQ, IDX_DIM), BF16),
            pltpu.VMEM((1, TQ), I32),
            pltpu.VMEM((1, TQ), I32),
            pltpu.VMEM((1, TQ), I32),
            pltpu.VMEM((1, TQ), I32),
            pltpu.VMEM((DSA_HEADS, 1, TQ), F32),
            pltpu.VMEM((DSA_HEADS, LANES, TQ), F32),
            pltpu.VMEM((DSA_HEADS, TQ, TQ), F32),
            pltpu.VMEM((DSA_HEADS, TQ, TQ), F32),
        ],
        compiler_params=_cparams(("arbitrary", "arbitrary")),
        name="dsa",
    )(rel_bias, z, z, z, z, z, z)


CONV_HALO = 32


def _conv_kernel(u_ref, w_ref, cb_ref, g_ref, b_ref, o_ref, h_ref, *, tm):
    @pl.when(pl.program_id(1) == 0)
    def _():
        h_ref[0:CONV_HALO, :] = jnp.zeros((CONV_HALO, CONV_CH), F32)

    u = u_ref[...].astype(F32)
    a = u[:, :CONV_CH]
    g = u[:, CONV_CH:]
    h_ref[CONV_HALO:CONV_HALO + tm, :] = a / (1.0 + jnp.exp(-g))
    acc = jnp.zeros((tm, CONV_CH), F32) + cb_ref[...]
    base = CONV_HALO - (CONV_WIDTH - 1)
    for j in range(CONV_WIDTH):
        acc = acc + h_ref[base + j:base + j + tm, :] * w_ref[j:j + 1, :]
    h_ref[0:CONV_HALO, :] = h_ref[tm:tm + CONV_HALO, :]
    mu = jnp.mean(acc, axis=-1, keepdims=True)
    d = acc - mu
    var = jnp.mean(d * d, axis=-1, keepdims=True)
    y = d * lax.rsqrt(var + NORM_EPS) * g_ref[...] + b_ref[...]
    o_ref[...] = (y / (1.0 + jnp.exp(-y))).astype(BF16)


def _conv(z, w, cb, g, b, B, S, tm):
    T = z.shape[0]
    nblk = S // tm
    const = lambda bb, c: (0, 0)
    return pl.pallas_call(
        functools.partial(_conv_kernel, tm=tm),
        out_shape=jax.ShapeDtypeStruct((T, CONV_CH), BF16),
        grid=(B, nblk),
        in_specs=[
            pl.BlockSpec((tm, CU_W), lambda bb, c: (bb * nblk + c, OFF_CU // CU_W)),
            pl.BlockSpec((CONV_WIDTH, CONV_CH), const),
            pl.BlockSpec((1, CONV_CH), const),
            pl.BlockSpec((1, CONV_CH), const),
            pl.BlockSpec((1, CONV_CH), const),
        ],
        out_specs=pl.BlockSpec((tm, CONV_CH), lambda bb, c: (bb * nblk + c, 0)),
        scratch_shapes=[pltpu.VMEM((tm + CONV_HALO, CONV_CH), F32)],
        compiler_params=_cparams(("arbitrary", "arbitrary")),
        name="conv",
    )(z, w, cb, g, b)


XW = X_HEADS * X_DH


def _mem_kv_kernel(m_ref, g_ref, w_ref, kn_ref, e_ref, k_ref, v_ref, *, mlen):
    hm = _rms(m_ref[...], g_ref[...]).astype(BF16)
    kv = _dot(hm, w_ref[...])
    k = kv[:, :XW]
    v = kv[:, XW:]
    ss = _dot_hi(k * k, e_ref[...]) * (1.0 / X_DH)
    k = k * lax.rsqrt(ss + NORM_EPS) * kn_ref[...] * (X_DH ** -0.5)
    lane_head = lax.broadcasted_iota(I32, (mlen, XW), 1) // X_DH
    for hh in range(X_HEADS):
        msk = lane_head == hh
        k_ref[hh * mlen:(hh + 1) * mlen, :] = jnp.where(msk, k, 0.0).astype(BF16)
        v_ref[hh * mlen:(hh + 1) * mlen, :] = jnp.where(msk, v, 0.0).astype(BF16)


def _mem_kv(mem2d, g, w_ckv, kn_t, e_x, B, M):
    D = mem2d.shape[1]
    const = lambda b: (0, 0)
    out = jax.ShapeDtypeStruct((B * X_HEADS * M, XW), BF16)
    return pl.pallas_call(
        functools.partial(_mem_kv_kernel, mlen=M),
        out_shape=(out, out),
        grid=(B,),
        in_specs=[
            pl.BlockSpec((M, D), lambda b: (b, 0)),
            pl.BlockSpec((1, D), const),
            pl.BlockSpec((D, 2 * XW), const),
            pl.BlockSpec((1, XW), const),
            pl.BlockSpec((XW, XW), const),
        ],
        out_specs=(pl.BlockSpec((X_HEADS * M, XW), lambda b: (b, 0)),
                   pl.BlockSpec((X_HEADS * M, XW), lambda b: (b, 0))),
        compiler_params=_cparams(("arbitrary",)),
        name="mem_kv",
    )(mem2d, g, w_ckv, kn_t, e_x)


def _post_kernel(x_ref, og_ref, od_ref, oc_ref, wo_ref, nx_ref, wcq_ref, qn_ref, e_ref,
                 kbd_ref, vbd_ref, obd_ref, wco_ref, nf_ref, wr_ref, br_ref,
                 x2_ref, hf_ref, gate_ref, *, mlen):
    wo = wo_ref
    x1 = (x_ref[...]
          + _dot(og_ref[...], wo[0:GLA_VW, :])
          + _dot(od_ref[...], wo[GLA_VW:2 * GLA_VW, :])
          + _dot(oc_ref[...], wo[2 * GLA_VW:, :]))
    hx = _rms(x1, nx_ref[...]).astype(BF16)
    q = _dot(hx, wcq_ref[...])
    ss = _dot_hi(q * q, e_ref[...]) * (1.0 / X_DH)
    q = (q * lax.rsqrt(ss + NORM_EPS) * qn_ref[...]).astype(BF16)
    logits = _dot_nt(q, kbd_ref[...])
    ps = []
    for hh in range(X_HEADS):
        lg = logits[:, hh * mlen:(hh + 1) * mlen]
        ps.append(jnp.exp(lg - jnp.max(lg, axis=-1, keepdims=True)))
    p = jnp.concatenate(ps, axis=-1).astype(BF16)
    o = _dot(p, vbd_ref[...]) / _dot(p, obd_ref[...])
    x2 = x1 + _dot(o.astype(BF16), wco_ref[...])
    x2_ref[...] = x2
    hf = _rms(x2, nf_ref[...])
    hf_ref[...] = hf.astype(BF16)
    rl = _dot_hi(hf, wr_ref[...]) + br_ref[...]
    tm = rl.shape[0]
    lane = lax.broadcasted_iota(I32, (tm, LANES), 1)
    is_g = (lane >= N_EXPERTS) & (lane < N_EXPERTS + N_GROUPS)
    g_max = jnp.max(jnp.where(is_g, rl, NEG), axis=-1, keepdims=True)
    g_sel = jnp.min(jnp.where(is_g & (rl == g_max), lane, LANES), axis=-1, keepdims=True) - N_EXPERTS
    g_den = jnp.sum(jnp.where(is_g, jnp.exp(rl - g_max), 0.0), axis=-1, keepdims=True)
    g_w = 1.0 / g_den
    in_grp = (lane < N_EXPERTS) & ((lane // EXPERTS_PER_GROUP) == g_sel)
    e1 = jnp.max(jnp.where(in_grp, rl, NEG), axis=-1, keepdims=True)
    l1 = jnp.min(jnp.where(in_grp & (rl == e1), lane, LANES), axis=-1, keepdims=True)
    rest = in_grp & (lane != l1)
    e2 = jnp.max(jnp.where(rest, rl, NEG), axis=-1, keepdims=True)
    l2 = jnp.min(jnp.where(rest & (rl == e2), lane, LANES), axis=-1, keepdims=True)
    p2 = jnp.exp(e2 - e1)
    w1 = 1.0 / (1.0 + p2)
    w2 = p2 / (1.0 + p2)
    gate_ref[...] = jnp.where(lane == l1, w1 * g_w, jnp.where(lane == l2, w2 * g_w, 0.0))


def _post(x2d, o_gla, o_dsa, o_conv, w_out, nx, w_cq, qn_t, e_x, kbd, vbd, obd, w_co, nf, w_r, b_r,
          B, S, M, tm):
    T, D = x2d.shape
    nblk = S // tm
    const = lambda i: (0, 0)
    row = lambda i: (i, 0)
    bat = lambda i: (i // nblk, 0)
    HM = X_HEADS * M
    return pl.pallas_call(
        functools.partial(_post_kernel, mlen=M),
        out_shape=(jax.ShapeDtypeStruct((T, D), F32), jax.ShapeDtypeStruct((T, D), BF16),
                   jax.ShapeDtypeStruct((T, LANES), F32)),
        grid=(T // tm,),
        in_specs=[
            pl.BlockSpec((tm, D), row),
            pl.BlockSpec((tm, GLA_VW), row),
            pl.BlockSpec((tm, DSA_HEADS * DSA_DH), row),
            pl.BlockSpec((tm, CONV_CH), row),
            pl.BlockSpec((D, D), const),
            pl.BlockSpec((1, D), const),
            pl.BlockSpec((D, XW), const),
            pl.BlockSpec((1, XW), const),
            pl.BlockSpec((XW, XW), const),
            pl.BlockSpec((HM, XW), bat),
            pl.BlockSpec((HM, XW), bat),
            pl.BlockSpec((HM, XW), const),
            pl.BlockSpec((XW, D), const),
            pl.BlockSpec((1, D), const),
            pl.BlockSpec((D, LANES), const),
            pl.BlockSpec((1, LANES), const),
        ],
        out_specs=(pl.BlockSpec((tm, D), row), pl.BlockSpec((tm, D), row),
                   pl.BlockSpec((tm, LANES), row)),
        compiler_params=_cparams(("arbitrary",)),
        name="post",
    )(x2d, o_gla, o_dsa, o_conv, w_out, nx, w_cq, qn_t, e_x, kbd, vbd, obd, w_co, nf, w_r, b_r)


def _moe_kernel(x2_ref, hf_ref, gate_ref, wg_ref, wu_ref, wd_ref, o_ref):
    e = pl.program_id(1)

    @pl.when(e == 0)
    def _():
        o_ref[...] = x2_ref[...]

    t = hf_ref[...]
    lane = lax.broadcasted_iota(I32, gate_ref.shape, 1)
    g = jnp.sum(jnp.where(lane == e, gate_ref[...], 0.0), axis=-1, keepdims=True)
    a = _dot(t, wg_ref[0])
    u = _dot(t, wu_ref[0])
    he = (a / (1.0 + jnp.exp(-a))) * u * g
    o_ref[...] += _dot(he.astype(BF16), wd_ref[0])


def _moe(x2, hf, gates, wg, wu, wd, tm):
    T, D = x2.shape
    DE = wg.shape[-1]
    row = lambda i, e: (i, 0)
    return pl.pallas_call(
        _moe_kernel,
        out_shape=jax.ShapeDtypeStruct((T, D), F32),
        grid=(T // tm, N_EXPERTS),
        in_specs=[
            pl.BlockSpec((tm, D), row),
            pl.BlockSpec((tm, D), row),
            pl.BlockSpec((tm, LANES), row),
            pl.BlockSpec((1, D, DE), lambda i, e: (e, 0, 0)),
            pl.BlockSpec((1, D, DE), lambda i, e: (e, 0, 0)),
            pl.BlockSpec((1, DE, D), lambda i, e: (e, 0, 0)),
        ],
        out_specs=pl.BlockSpec((tm, D), row),
        compiler_params=_cparams(("arbitrary", "arbitrary")),
        name="moe",
    )(x2, hf, gates, wg, wu, wd)


def _pad_heads(w, heads, dh, slot):
    lead = w.shape[:-1]
    w = w.reshape(lead + (heads, dh))
    w = jnp.pad(w, [(0, 0)] * len(lead) + [(0, 0), (0, slot - dh)])
    return w.reshape(lead + (heads * slot,))


def _pad_cols(w, width):
    return jnp.pad(w, [(0, 0)] * (w.ndim - 1) + [(0, width - w.shape[-1])])


def _layout_w_in(w):
    sizes = (GLA_HEADS * GLA_DK, GLA_HEADS * GLA_DK, GLA_VW, GLA_RANK, GLA_VW,
             DSA_HEADS * DSA_DH, DSA_HEADS * DSA_DH, DSA_HEADS * DSA_DH, IQ_W, IDX_DIM, IDX_HEADS,
             CU_W)
    offs = np.cumsum((0,) + sizes)
    g_q, g_k, g_v, g_lr, g_r, d_q, d_k, d_v, i_q, i_k, i_w, c_u = [
        w[:, offs[n]:offs[n + 1]] for n in range(len(sizes))]
    cols = [
        _pad_heads(d_q, DSA_HEADS, DSA_DH, LANES),
        _pad_heads(d_k, DSA_HEADS, DSA_DH, LANES),
        _pad_heads(d_v, DSA_HEADS, DSA_DH, LANES),
        g_v, g_r,
        _pad_heads(g_q, GLA_HEADS, GLA_DK, GLA_DKP),
        _pad_heads(g_k, GLA_HEADS, GLA_DK, GLA_DKP),
        i_q, c_u,
        _pad_cols(g_lr, LANES),
        _pad_cols(jnp.concatenate([i_k, i_w], axis=-1), LANES),
    ]
    out = jnp.concatenate(cols, axis=-1)
    assert out.shape[-1] == NP_IN
    return out.astype(BF16)


def _block_ones(n, blk):
    idx = np.arange(n) // blk
    return jnp.asarray((idx[:, None] == idx[None, :]).astype(np.float32))


def _layer(x2d, mem2d, rel_bias, p, B, S, M):
    D = x2d.shape[1]
    row = lambda v: v.reshape(1, -1)
    ones_dv = np.zeros((1, DSA_W), np.float32)
    ones_dv[0, DSA_DH::LANES] = 1.0
    z = _in_proj(x2d, row(p["norm_mix"]), _layout_w_in(p["w_in"]),
                 _pad_cols(row(p["dsa_qnorm"]), LANES), _pad_cols(row(p["dsa_knorm"]), LANES),
                 jnp.asarray(ones_dv), tm=min(512, S))
    wa2_p = jnp.pad(_pad_heads(p["gla_wa2"], GLA_HEADS, GLA_DK, GLA_DKP),
                    ((0, LANES - GLA_RANK), (0, 0)))
    o_gla = _gla(z, wa2_p, row(_pad_heads(p["gla_ba"], GLA_HEADS, GLA_DK, GLA_DKP)),
                 row(jnp.tile(p["gla_onorm"], GLA_HEADS)), B, S, rows=min(256, S))
    o_dsa = _dsa(z, rel_bias, B, S)
    o_conv = _conv(z, p["conv_w"], row(p["conv_b"]), row(p["conv_ln_g"]), row(p["conv_ln_b"]),
                   B, S, tm=min(512, S))
    e_x = _block_ones(XW, X_DH)
    kbd, vbd = _mem_kv(mem2d, row(p["norm_mem"]), p["w_ckv"].astype(BF16),
                       row(jnp.tile(p["x_knorm"], X_HEADS)), e_x, B, M)
    obd = (jnp.arange(X_HEADS * M)[:, None] // M == jnp.arange(XW)[None, :] // X_DH).astype(BF16)
    w_r = _pad_cols(jnp.concatenate([p["w_re"], p["w_rg"]], axis=-1), LANES)
    b_r = _pad_cols(row(jnp.concatenate([p["b_re"], p["b_rg"]])), LANES)
    x2, hf, gates = _post(
        x2d, o_gla, o_dsa, o_conv, p["w_out"].astype(BF16), row(p["norm_x"]),
        p["w_cq"].astype(BF16), row(jnp.tile(p["x_qnorm"], X_HEADS)), e_x, kbd, vbd, obd,
        p["w_co"].astype(BF16), row(p["norm_ffn"]), w_r, b_r, B, S, M, tm=min(256, S))
    return _moe(x2, hf, gates, p["w_gate"].astype(BF16), p["w_up"].astype(BF16),
                p["w_down"].astype(BF16), tm=min(1024, S))


_LAYER_KEYS = ("norm_mix", "w_in", "gla_wa2", "gla_ba", "gla_onorm", "dsa_qnorm", "dsa_knorm",
               "conv_w", "conv_b", "conv_ln_g", "conv_ln_b", "w_out", "norm_x", "norm_mem", "w_cq",
               "w_ckv", "x_qnorm", "x_knorm", "w_co", "norm_ffn", "w_rg", "b_rg", "w_re", "b_re",
               "w_gate", "w_up", "w_down")


def kernel(x, mem, rel_bias, norm_mix, w_in, gla_wa2, gla_ba, gla_onorm, dsa_qnorm, dsa_knorm,
           conv_w, conv_b, conv_ln_g, conv_ln_b, w_out, norm_x, norm_mem, w_cq, w_ckv, x_qnorm,
           x_knorm, w_co, norm_ffn, w_rg, b_rg, w_re, b_re, w_gate, w_up, w_down):
    stacked = dict(zip(_LAYER_KEYS, (
        norm_mix, w_in, gla_wa2, gla_ba, gla_onorm, dsa_qnorm, dsa_knorm, conv_w, conv_b,
        conv_ln_g, conv_ln_b, w_out, norm_x, norm_mem, w_cq, w_ckv, x_qnorm, x_knorm, w_co,
        norm_ffn, w_rg, b_rg, w_re, b_re, w_gate, w_up, w_down)))
    B, S, D = x.shape
    M = mem.shape[1]
    x2d = x.reshape(B * S, D)
    mem2d = mem.reshape(B * M, D)
    for l in range(norm_mix.shape[0]):
        x2d = _layer(x2d, mem2d, rel_bias, {k: v[l] for k, v in stacked.items()}, B, S, M)
    return x2d.reshape(B, S, D)
```

```python
import functools
import math

import jax
import jax.numpy as jnp
import numpy as np
from jax import lax
from jax.experimental import pallas as pl
from jax.experimental.pallas import tpu as pltpu

F32 = jnp.float32
BF16 = jnp.bfloat16
I32 = jnp.int32

LANES = 128
NORM_EPS = 1e-6

GLA_HEADS = 4
GLA_DK = 48
GLA_DV = 96
GLA_DKP = 64
GLA_RANK = 16
GLA_TAU = 16.0
DSA_HEADS = 6
DSA_DH = 64
IDX_HEADS = 8
IDX_DIM = 64
TOPK_MAX = 256
CONV_CH = 256
CONV_WIDTH = 31
NUM_BUCKETS = 32
MAX_DISTANCE = 128
X_HEADS = 4
X_DH = 64
N_GROUPS = 4
EXPERTS_PER_GROUP = 4
N_EXPERTS = 16
MOE_PAIRS = 6
MOE_BUCKETS = N_GROUPS * MOE_PAIRS

DSA_W = DSA_HEADS * LANES
GLA_VW = GLA_HEADS * GLA_DV
GLA_KW = GLA_HEADS * GLA_DKP
IQ_W = IDX_HEADS * IDX_DIM
CU_W = 2 * CONV_CH
OFF_DQ, OFF_DK, OFF_DV = 0, DSA_W, 2 * DSA_W
OFF_GV = 3 * DSA_W
OFF_GR = OFF_GV + GLA_VW
OFF_GQ = OFF_GR + GLA_VW
OFF_GK = OFF_GQ + GLA_KW
OFF_IQ = OFF_GK + GLA_KW
OFF_CU = OFF_IQ + IQ_W
OFF_LR = OFF_CU + CU_W
OFF_IK = OFF_LR + LANES
NP_IN = OFF_IK + LANES

NEG = -1e30
LOG2E = math.log2(math.e)
INT_MIN = -2 ** 31
INT_MAX = 2 ** 31 - 1

VMEM_LIMIT = 56 * 1024 * 1024


def _cparams(sem):
    return pltpu.CompilerParams(dimension_semantics=sem, vmem_limit_bytes=VMEM_LIMIT)


def _dot(a, b):
    return jnp.dot(a, b, preferred_element_type=F32)


def _dot_nt(a, b):
    return lax.dot_general(a, b, (((1,), (1,)), ((), ())), preferred_element_type=F32)


def _split_bf16(x, terms):
    parts = []
    for _ in range(terms):
        hi = x.astype(BF16)
        parts.append(hi)
        x = x - hi.astype(F32)
    return parts


def _dot_exact_lhs(a_bf16, b, terms):
    return functools.reduce(lambda u, v: u + v, [_dot(a_bf16, piece) for piece in _split_bf16(b, terms)])


def _dot_exact_rhs(a, b_bf16, terms):
    return functools.reduce(lambda u, v: u + v, [_dot(piece, b_bf16) for piece in _split_bf16(a, terms)])


def _rms(x, g):
    return x * lax.rsqrt(jnp.mean(x * x, axis=-1, keepdims=True) + NORM_EPS) * g


_IN_CHUNKS = ((OFF_DQ, DSA_W), (OFF_DK, DSA_W), (OFF_DV, DSA_W), (OFF_GV, 2 * GLA_VW),
              (OFF_GQ, 2 * GLA_KW), (OFF_IQ, IQ_W), (OFF_CU, CU_W), (OFF_LR, 2 * LANES))


def _in_proj_kernel(x_ref, g_ref, w_ref, qn_ref, kn_ref, ones_ref, o_ref):
    h = _rms(x_ref[...], g_ref[...]).astype(BF16)

    def head_norm(y, gain, scale):
        parts = []
        for hh in range(DSA_HEADS):
            yh = y[:, hh * LANES:(hh + 1) * LANES]
            ss = jnp.sum(yh * yh, axis=-1, keepdims=True) * (1.0 / DSA_DH)
            parts.append(yh * lax.rsqrt(ss + NORM_EPS) * gain * scale)
        return jnp.concatenate(parts, axis=-1)

    for c0, cw in _IN_CHUNKS:
        y = _dot(h, w_ref[:, c0:c0 + cw])
        if c0 == OFF_DQ:
            y = head_norm(y, qn_ref[...], DSA_DH ** -0.5 * LOG2E)
        elif c0 == OFF_DK:
            y = head_norm(y, kn_ref[...], 1.0)
        elif c0 == OFF_DV:
            y = y + ones_ref[...]
        o_ref[:, c0:c0 + cw] = y.astype(BF16)


def _in_proj(x2d, g, w_p, qn_p, kn_p, ones_p, tm):
    T, D = x2d.shape
    return pl.pallas_call(
        _in_proj_kernel,
        out_shape=jax.ShapeDtypeStruct((T, NP_IN), BF16),
        grid=(T // tm,),
        in_specs=[
            pl.BlockSpec((tm, D), lambda i: (i, 0)),
            pl.BlockSpec((1, D), lambda i: (0, 0)),
            pl.BlockSpec((D, NP_IN), lambda i: (0, 0)),
            pl.BlockSpec((1, LANES), lambda i: (0, 0)),
            pl.BlockSpec((1, LANES), lambda i: (0, 0)),
            pl.BlockSpec((1, DSA_W), lambda i: (0, 0)),
        ],
        out_specs=pl.BlockSpec((tm, NP_IN), lambda i: (i, 0)),
        compiler_params=_cparams(("arbitrary",)),
        name="in_proj",
    )(x2d, g, w_p, qn_p, kn_p, ones_p)


GLA_C = 64
GLA_SUB = 16
GLA_NSUB = GLA_C // GLA_SUB


def _gla_kernel(q_ref, k_ref, v_ref, r_ref, lr_ref, wa2_ref, ba_ref, on_ref, e_ref, e2_ref,
                et_ref, tril_ref, o_ref, st_ref, qf_ref, kf_ref, vf_ref, bf_ref, of_ref, *, rows):
    c = pl.program_id(1)

    @pl.when(c == 0)
    def _():
        st_ref[...] = jnp.zeros_like(st_ref)

    lr = lr_ref[...]
    pre = _dot(lr, wa2_ref[0]) + _dot(lr, wa2_ref[1]) + ba_ref[...]
    log_a = (jnp.minimum(pre, 0.0) - jnp.log(1.0 + jnp.exp(-jnp.abs(pre)))) * (LOG2E / GLA_TAU)
    bf_ref[...] = _dot_exact_lhs(tril_ref[...], log_a, 3)
    qf_ref[...] = q_ref[...].astype(F32) * (GLA_DK ** -0.5)
    kf_ref[...] = k_ref[...].astype(F32)
    vf_ref[...] = v_ref[...].astype(F32)
    e_mat = e_ref[...]
    row_id = lax.broadcasted_iota(I32, (GLA_SUB, GLA_KW), 0)

    def chunk(ci, carry):
        r0 = pl.multiple_of(ci * GLA_C, GLA_C)
        q = qf_ref[pl.ds(r0, GLA_C), :]
        k = kf_ref[pl.ds(r0, GLA_C), :]
        v = vf_ref[pl.ds(r0, GLA_C), :]
        b = bf_ref[pl.ds(r0, GLA_C), :]
        st = st_ref[...]
        o_inter = _dot_nt((q * jnp.exp2(b)).astype(BF16), st.astype(BF16))
        b_last = b[GLA_C - 1:GLA_C, :]
        kd = (k * jnp.exp2(b_last - b)).astype(BF16)
        upd = _dot(v.T.astype(BF16), kd) * et_ref[...]
        st_ref[...] = st * jnp.exp2(b_last) + upd
        outs = []
        for i in range(GLA_NSUB):
            qi = q[i * GLA_SUB:(i + 1) * GLA_SUB, :]
            bi = b[i * GLA_SUB:(i + 1) * GLA_SUB, :]
            acc = o_inter[i * GLA_SUB:(i + 1) * GLA_SUB, :]
            for j in range(i + 1):
                slabs = []
                for s in range(GLA_SUB):
                    row = r0 + (j * GLA_SUB + s)
                    ks = kf_ref[pl.ds(row, 1), :]
                    bs = bf_ref[pl.ds(row, 1), :]
                    if i == j:
                        p = jnp.where(row_id >= s, qi * ks * jnp.exp2(jnp.minimum(bi - bs, 0.0)), 0.0)
                    else:
                        p = qi * ks * jnp.exp2(bi - bs)
                    slabs.append(p.astype(BF16))
                pr = _dot(jnp.concatenate(slabs, axis=0), e_mat)
                for s in range(GLA_SUB):
                    vs = vf_ref[pl.ds(r0 + (j * GLA_SUB + s), 1), :]
                    acc = acc + pr[s * GLA_SUB:(s + 1) * GLA_SUB, :] * vs
            outs.append(acc)
        of_ref[pl.ds(r0, GLA_C), :] = jnp.concatenate(outs, axis=0)
        return carry

    lax.fori_loop(0, rows // GLA_C, chunk, 0)

    o = of_ref[...]
    ss = _dot_exact_rhs(o * o, e2_ref[...], 2) * (1.0 / GLA_DV)
    o = o * lax.rsqrt(ss + NORM_EPS) * on_ref[...]
    r = r_ref[...].astype(F32)
    o_ref[...] = (o * (r / (1.0 + jnp.exp(-r)))).astype(BF16)


def _gla(z, wa2_p, ba_p, on_p, B, S, rows):
    T = z.shape[0]
    nblk = S // rows
    e = np.zeros((GLA_KW, GLA_VW), np.float32)
    for h in range(GLA_HEADS):
        e[h * GLA_DKP:(h + 1) * GLA_DKP, h * GLA_DV:(h + 1) * GLA_DV] = 1.0
    e2 = np.zeros((GLA_VW, GLA_VW), np.float32)
    for h in range(GLA_HEADS):
        e2[h * GLA_DV:(h + 1) * GLA_DV, h * GLA_DV:(h + 1) * GLA_DV] = 1.0
    tr = np.zeros((rows, rows), np.float32)
    for cc in range(rows // GLA_C):
        tr[cc * GLA_C:(cc + 1) * GLA_C, cc * GLA_C:(cc + 1) * GLA_C] = np.tril(np.ones((GLA_C, GLA_C)))
    rowmap = lambda b, c: (b * nblk + c)
    const = lambda b, c: (0, 0)
    return pl.pallas_call(
        functools.partial(_gla_kernel, rows=rows),
        out_shape=jax.ShapeDtypeStruct((T, GLA_VW), BF16),
        grid=(B, nblk),
        in_specs=[
            pl.BlockSpec((rows, GLA_KW), lambda b, c: (rowmap(b, c), OFF_GQ // GLA_KW)),
            pl.BlockSpec((rows, GLA_KW), lambda b, c: (rowmap(b, c), OFF_GK // GLA_KW)),
            pl.BlockSpec((rows, GLA_VW), lambda b, c: (rowmap(b, c), OFF_GV // GLA_VW)),
            pl.BlockSpec((rows, GLA_VW), lambda b, c: (rowmap(b, c), OFF_GR // GLA_VW)),
            pl.BlockSpec((rows, LANES), lambda b, c: (rowmap(b, c), OFF_LR // LANES)),
            pl.BlockSpec((2, LANES, GLA_KW), lambda b, c: (0, 0, 0)),
            pl.BlockSpec((1, GLA_KW), const),
            pl.BlockSpec((1, GLA_VW), const),
            pl.BlockSpec((GLA_KW, GLA_VW), const),
            pl.BlockSpec((GLA_VW, GLA_VW), const),
            pl.BlockSpec((GLA_VW, GLA_KW), const),
            pl.BlockSpec((rows, rows), const),
        ],
        out_specs=pl.BlockSpec((rows, GLA_VW), lambda b, c: (rowmap(b, c), 0)),
        scratch_shapes=[
            pltpu.VMEM((GLA_VW, GLA_KW), F32),
            pltpu.VMEM((rows, GLA_KW), F32),
            pltpu.VMEM((rows, GLA_KW), F32),
            pltpu.VMEM((rows, GLA_VW), F32),
            pltpu.VMEM((rows, GLA_KW), F32),
            pltpu.VMEM((rows, GLA_VW), F32),
        ],
        compiler_params=_cparams(("arbitrary", "arbitrary")),
        name="gla",
    )(z, z, z, z, z, wa2_p, ba_p, on_p, jnp.asarray(e, BF16), jnp.asarray(e2, BF16), jnp.asarray(e.T),
      jnp.asarray(tr, BF16))


DSA_TQ = 256


def _t5_bucket(dist):
    max_exact = NUM_BUCKETS // 2
    d = jnp.maximum(dist, 1).astype(F32)
    large = max_exact + (jnp.log(d / max_exact) / math.log(MAX_DISTANCE / max_exact)
                         * (NUM_BUCKETS - max_exact)).astype(I32)
    large = jnp.minimum(large, NUM_BUCKETS - 1)
    return jnp.where(dist < max_exact, dist, large)


def _block_loop(start, stop, body, init):
    npair = (stop - start) // 2
    carry = lax.fori_loop(0, npair, lambda i, c: body([start + 2 * i, start + 2 * i + 1], c), init)
    return lax.cond((stop - start) % 2 == 1, lambda c: body([stop - 1], c), lambda c: c, carry)


def _dsa_kernel(rb_ref, dq_ref, dk_ref, dv_ref, iq_ref, ik_ref, iw_ref, o_ref,
                bias_ref, key_ref, key16_ref, vt_ref, iqs_ref, lo_ref, hi_ref, clo_ref, chi_ref,
                m_ref, acc_ref, lga_ref, lgb_ref, *, seq, topk):
    TQ = DSA_TQ
    bi = pl.program_id(0)
    qb = pl.program_id(1)
    nkb = qb + 1
    log2_seq = int(math.log2(seq))

    s_row = lax.broadcasted_iota(I32, (TQ, TQ), 0)
    t_lane = lax.broadcasted_iota(I32, (TQ, TQ), 1)

    @pl.when((bi == 0) & (qb == 0))
    def _():
        for delta in range(2):
            bucket = _t5_bucket(jnp.maximum(delta * TQ + t_lane - s_row, 0))
            for hh in range(DSA_HEADS):
                tile = jnp.zeros((TQ, TQ), F32)
                for bk in range(NUM_BUCKETS):
                    tile = jnp.where(bucket == bk, rb_ref[bk, hh] * LOG2E, tile)
                bias_ref[hh, delta] = tile

    @pl.when(qb == 0)
    def _():
        def tr(kb, carry):
            k0 = pl.multiple_of(kb * TQ, TQ)
            for hh in range(DSA_HEADS):
                blk = dv_ref[pl.ds(k0, TQ), hh * LANES:(hh + 1) * LANES].astype(F32)
                vt_ref[kb, hh] = blk.T.astype(BF16)
            return carry

        lax.fori_loop(0, seq // TQ, tr, 0)

    for hh in range(IDX_HEADS):
        iqs_ref[hh * TQ:(hh + 1) * TQ, :] = iq_ref[:, hh * IDX_DIM:(hh + 1) * IDX_DIM]
    w_t = iw_ref[...].astype(F32).T * ((IDX_HEADS ** -0.5) * (IDX_DIM ** -0.5))

    def score_blocks(kbs, carry):
        k0s = [pl.multiple_of(kb * TQ, TQ) for kb in kbs]
        dots = [_dot_nt(ik_ref[pl.ds(k0, TQ), :][:, :IDX_DIM], iqs_ref[...]) for k0 in k0s]
        for kb, k0, dt in zip(kbs, k0s, dots):
            sc = jnp.zeros((TQ, TQ), F32)
            for hh in range(IDX_HEADS):
                sc = sc + (jnp.maximum(dt[:, hh * TQ:(hh + 1) * TQ], 0.0)
                           * w_t[IDX_DIM + hh:IDX_DIM + hh + 1, :])
            bits = pltpu.bitcast(sc, I32)
            keys = bits ^ ((bits >> 31) & INT_MAX)
            causal = (k0 + s_row) <= (qb * TQ + t_lane)
            keys = jnp.where(causal, keys, INT_MIN)
            key_ref[kb] = keys
            key16_ref[kb] = (keys >> 16).astype(jnp.int16)
        return carry

    _block_loop(0, nkb, score_blocks, 0)

    ncausal = qb * TQ + lax.broadcasted_iota(I32, (1, TQ), 1) + 1
    short = ncausal <= topk

    def count_cols(pred, dtype, ref):
        rows = 8 * (4 // jnp.dtype(dtype).itemsize)

        def body(kbs, cnt):
            for kb in kbs:
                hit = jnp.where(pred(ref[kb], kb), jnp.ones((), dtype), jnp.zeros((), dtype))
                for r in range(TQ // rows):
                    cnt = cnt + hit[r * rows:(r + 1) * rows, :]
            return cnt

        cnt = _block_loop(0, nkb, body, jnp.zeros((rows, TQ), dtype))
        return jnp.sum(cnt.astype(I32), axis=0, keepdims=True)

    def bisect_step(state, count_ge):
        lo, hi, clo, chi = state
        mid = (lo >> 1) + (hi >> 1) + (lo & hi & 1)
        c = count_ge(mid)
        up = (c >= topk) & (mid != lo)
        down = c < topk
        return (jnp.where(up, mid, lo), jnp.where(down, mid, hi),
                jnp.where(up, c, clo), jnp.where(down, c, chi))

    def step16(_, state):
        return bisect_step(state, lambda mid: count_cols(
            lambda kk, kb: kk >= mid.astype(jnp.int16), jnp.int16, key16_ref))

    lo16, _, clo, chi = lax.fori_loop(
        0, 16, step16,
        (jnp.full((1, TQ), -2 ** 15 + 1, I32), jnp.full((1, TQ), 2 ** 15, I32),
         ncausal, jnp.zeros((1, TQ), I32)))
    lo16_h = lo16.astype(jnp.int16)

    def low_half(kbs, carry):
        for kb in kbs:
            low = ((key_ref[kb] & 0xFFFF) - 2 ** 15).astype(jnp.int16)
            key16_ref[kb] = jnp.where(key16_ref[kb] == lo16_h, low, jnp.int16(-2 ** 15))
        return carry

    _block_loop(0, nkb, low_half, 0)
    lo_ref[...] = jnp.full((1, TQ), -2 ** 15, I32)
    hi_ref[...] = jnp.full((1, TQ), 2 ** 15, I32)
    clo_ref[...] = clo
    chi_ref[...] = chi
    above = chi

    def active_count():
        open_q = ((clo_ref[...] != topk) & (hi_ref[...] - lo_ref[...] > 1)
                  & jnp.logical_not(short))
        return jnp.sum(open_q.astype(I32))

    def bis_body(state):
        it, _ = state
        st = (lo_ref[...], hi_ref[...], clo_ref[...], chi_ref[...])
        for _ in range(2):
            frozen = st[2] == topk
            new = bisect_step(st, lambda mid: above + count_cols(
                lambda kk, kb: kk >= mid.astype(jnp.int16), jnp.int16, key16_ref))
            st = tuple(jnp.where(frozen, o, n) for o, n in zip(st, new))
        lo_ref[...], hi_ref[...], clo_ref[...], chi_ref[...] = st
        return it + 1, active_count()

    lax.while_loop(lambda s_: (s_[0] < 9) & (s_[1] > 0), bis_body, (jnp.int32(0), active_count()))
    lo_ref[...] = jnp.where(short, INT_MIN + 1, (lo16 << 16) + (lo_ref[...] + 2 ** 15))

    tie_q = (clo_ref[...] > topk) & jnp.logical_not(short)

    @pl.when(jnp.sum(tie_q.astype(I32)) > 0)
    def _():
        thr = lo_ref[...]
        need = topk - chi_ref[...]
        plo = jnp.full((1, TQ), -1, I32)
        phi = jnp.full((1, TQ), seq - 1, I32)
        for _ in range(log2_seq):
            pm = (plo + phi) >> 1
            c = count_cols(lambda kk, kb: (kk == thr) & ((kb * TQ + s_row) <= pm), I32, key_ref)
            ok = c >= need
            phi = jnp.where(ok, pm, phi)
            plo = jnp.where(ok, plo, pm)
        cut = jnp.where(tie_q, phi, seq - 1)

        def demote(kb, carry):
            kk = key_ref[kb]
            drop = (kk == thr) & ((kb * TQ + s_row) > cut)
            key_ref[kb] = jnp.where(drop, INT_MIN, kk)
            return carry

        lax.fori_loop(0, nkb, demote, 0)

    m_ref[...] = jnp.full(m_ref.shape, NEG, F32)
    acc_ref[...] = jnp.zeros_like(acc_ref)
    thr_sel = lo_ref[...]

    heads = range(DSA_HEADS)

    def qk(kb, hh):
        k0 = pl.multiple_of(kb * TQ, TQ)
        return _dot_nt(dk_ref[pl.ds(k0, TQ), hh * LANES:(hh + 1) * LANES],
                       dq_ref[:, hh * LANES:(hh + 1) * LANES])

    def softmax_update(kb, logits, shift):
        sel = key_ref[kb] >= thr_sel
        m_prev = [m_ref[hh] for hh in heads]
        acc_prev = [acc_ref[hh] for hh in heads]
        lg = [jnp.where(sel, logits[hh], NEG) for hh in heads]
        m_new = [jnp.maximum(m_prev[hh], jnp.max(lg[hh], axis=0, keepdims=True) + shift[hh])
                 for hh in heads]
        p = [jnp.exp2(lg[hh] - (m_new[hh] - shift[hh])).astype(BF16) for hh in heads]
        pv = [_dot(vt_ref[kb, hh], p[hh]) for hh in heads]
        for hh in heads:
            acc_ref[hh] = acc_prev[hh] * jnp.exp2(m_prev[hh] - m_new[hh]) + pv[hh]
            m_ref[hh] = m_new[hh]

    n_far = jnp.maximum(qb - 1, 0)
    far_shift = [rb_ref[NUM_BUCKETS - 1, hh] * LOG2E for hh in heads]

    @pl.when(n_far > 0)
    def _():
        for hh in heads:
            lga_ref[hh] = qk(0, hh)

    def far_pair(i, carry):
        ka = 2 * i
        for hh in heads:
            lgb_ref[hh] = qk(ka + 1, hh)
        softmax_update(ka, [lga_ref[hh] for hh in heads], far_shift)
        for hh in heads:
            lga_ref[hh] = qk(jnp.minimum(ka + 2, n_far - 1), hh)
        softmax_update(ka + 1, [lgb_ref[hh] for hh in heads], far_shift)
        return carry

    lax.fori_loop(0, n_far // 2, far_pair, 0)

    @pl.when(n_far % 2 == 1)
    def _():
        softmax_update(n_far - 1, [lga_ref[hh] for hh in heads], far_shift)

    def near_body(kb, carry):
        softmax_update(kb, [qk(kb, hh) + bias_ref[hh, qb - kb] for hh in heads], [0.0] * DSA_HEADS)
        return carry

    lax.fori_loop(n_far, nkb, near_body, 0)

    outs = []
    for hh in range(DSA_HEADS):
        a = acc_ref[hh]
        outs.append(a[:DSA_DH, :] / a[DSA_DH:DSA_DH + 1, :])
    o_ref[...] = jnp.concatenate(outs, axis=0).T.astype(BF16)


def _dsa(z, rel_bias, B, S):
    T = z.shape[0]
    TQ = DSA_TQ
    nqb = S // TQ
    topk = min(TOPK_MAX, S // 4)
    qrow = lambda b, q: b * nqb + q
    return pl.pallas_call(
        functools.partial(_dsa_kernel, seq=S, topk=topk),
        out_shape=jax.ShapeDtypeStruct((T, DSA_HEADS * DSA_DH), BF16),
        grid=(B, nqb),
        in_specs=[
            pl.BlockSpec(memory_space=pltpu.SMEM),
            pl.BlockSpec((TQ, DSA_W), lambda b, q: (qrow(b, q), OFF_DQ // DSA_W)),
            pl.BlockSpec((S, DSA_W), lambda b, q: (b, OFF_DK // DSA_W)),
            pl.BlockSpec((S, DSA_W), lambda b, q: (b, OFF_DV // DSA_W)),
            pl.BlockSpec((TQ, IQ_W), lambda b, q: (qrow(b, q), OFF_IQ // IQ_W)),
            pl.BlockSpec((S, LANES), lambda b, q: (b, OFF_IK // LANES)),
            pl.BlockSpec((TQ, LANES), lambda b, q: (qrow(b, q), OFF_IK // LANES)),
        ],
        out_specs=pl.BlockSpec((TQ, DSA_HEADS * DSA_DH), lambda b, q: (qrow(b, q), 0)),
        scratch_shapes=[
            pltpu.VMEM((DSA_HEADS, 2, TQ, TQ), F32),
            pltpu.VMEM((S // TQ, TQ, TQ), I32),
            pltpu.VMEM((S // TQ, TQ, TQ), jnp.int16),
            pltpu.VMEM((S // TQ, DSA_HEADS, LANES, TQ), BF16),
            pltpu.VMEM((IDX_HEADS * TQ, IDX_DIM), BF16),
            pltpu.VMEM((1, TQ), I32),
            pltpu.VMEM((1, TQ), I32),
            pltpu.VMEM((1, TQ), I32),
            pltpu.VMEM((1, TQ), I32),
            pltpu.VMEM((DSA_HEADS, 1, TQ), F32),
            pltpu.VMEM((DSA_HEADS, LANES, TQ), F32),
            pltpu.VMEM((DSA_HEADS, TQ, TQ), F32),
            pltpu.VMEM((DSA_HEADS, TQ, TQ), F32),
        ],
        compiler_params=_cparams(("arbitrary", "arbitrary")),
        name="dsa",
    )(rel_bias, z, z, z, z, z, z)


CONV_HALO = 32


def _conv_kernel(u_ref, w_ref, cb_ref, g_ref, b_ref, o_ref, h_ref, *, tm):
    @pl.when(pl.program_id(1) == 0)
    def _():
        h_ref[0:CONV_HALO, :] = jnp.zeros((CONV_HALO, CONV_CH), F32)

    u = u_ref[...].astype(F32)
    a = u[:, :CONV_CH]
    g = u[:, CONV_CH:]
    h_ref[CONV_HALO:CONV_HALO + tm, :] = a / (1.0 + jnp.exp(-g))
    acc = jnp.zeros((tm, CONV_CH), F32) + cb_ref[...]
    base = CONV_HALO - (CONV_WIDTH - 1)
    for j in range(CONV_WIDTH):
        acc = acc + h_ref[base + j:base + j + tm, :] * w_ref[j:j + 1, :]
    h_ref[0:CONV_HALO, :] = h_ref[tm:tm + CONV_HALO, :]
    mu = jnp.mean(acc, axis=-1, keepdims=True)
    d = acc - mu
    var = jnp.mean(d * d, axis=-1, keepdims=True)
    y = d * lax.rsqrt(var + NORM_EPS) * g_ref[...] + b_ref[...]
    o_ref[...] = (y / (1.0 + jnp.exp(-y))).astype(BF16)


def _conv(z, w, cb, g, b, B, S, tm):
    T = z.shape[0]
    nblk = S // tm
    const = lambda bb, c: (0, 0)
    return pl.pallas_call(
        functools.partial(_conv_kernel, tm=tm),
        out_shape=jax.ShapeDtypeStruct((T, CONV_CH), BF16),
        grid=(B, nblk),
        in_specs=[
            pl.BlockSpec((tm, CU_W), lambda bb, c: (bb * nblk + c, OFF_CU // CU_W)),
            pl.BlockSpec((CONV_WIDTH, CONV_CH), const),
            pl.BlockSpec((1, CONV_CH), const),
            pl.BlockSpec((1, CONV_CH), const),
            pl.BlockSpec((1, CONV_CH), const),
        ],
        out_specs=pl.BlockSpec((tm, CONV_CH), lambda bb, c: (bb * nblk + c, 0)),
        scratch_shapes=[pltpu.VMEM((tm + CONV_HALO, CONV_CH), F32)],
        compiler_params=_cparams(("arbitrary", "arbitrary")),
        name="conv",
    )(z, w, cb, g, b)


XW = X_HEADS * X_DH


def _mem_kv_kernel(m_ref, g_ref, w_ref, kn_ref, e_ref, k_ref, v_ref, *, mlen):
    hm = _rms(m_ref[...], g_ref[...]).astype(BF16)
    kv = _dot(hm, w_ref[...])
    k = kv[:, :XW]
    v = kv[:, XW:]
    ss = _dot_exact_rhs(k * k, e_ref[...], 2) * (1.0 / X_DH)
    k = k * lax.rsqrt(ss + NORM_EPS) * kn_ref[...] * (X_DH ** -0.5)
    lane_head = lax.broadcasted_iota(I32, (mlen, XW), 1) // X_DH
    for hh in range(X_HEADS):
        msk = lane_head == hh
        k_ref[hh * mlen:(hh + 1) * mlen, :] = jnp.where(msk, k, 0.0).astype(BF16)
        v_ref[hh * mlen:(hh + 1) * mlen, :] = jnp.where(msk, v, 0.0).astype(BF16)


def _mem_kv(mem2d, g, w_ckv, kn_t, e_x, B, M):
    D = mem2d.shape[1]
    const = lambda b: (0, 0)
    out = jax.ShapeDtypeStruct((B * X_HEADS * M, XW), BF16)
    return pl.pallas_call(
        functools.partial(_mem_kv_kernel, mlen=M),
        out_shape=(out, out),
        grid=(B,),
        in_specs=[
            pl.BlockSpec((M, D), lambda b: (b, 0)),
            pl.BlockSpec((1, D), const),
            pl.BlockSpec((D, 2 * XW), const),
            pl.BlockSpec((1, XW), const),
            pl.BlockSpec((XW, XW), const),
        ],
        out_specs=(pl.BlockSpec((X_HEADS * M, XW), lambda b: (b, 0)),
                   pl.BlockSpec((X_HEADS * M, XW), lambda b: (b, 0))),
        compiler_params=_cparams(("arbitrary",)),
        name="mem_kv",
    )(mem2d, g, w_ckv, kn_t, e_x)


def _post_kernel(x_ref, og_ref, od_ref, oc_ref, wo_ref, nx_ref, wcq_ref, qn_ref, e_ref,
                 kbd_ref, vbd_ref, obd_ref, wco_ref, nf_ref, wr_ref, br_ref,
                 x2_ref, hf_ref, route_ref, *, mlen):
    wo = wo_ref
    x1 = (x_ref[...]
          + _dot(og_ref[...], wo[0:GLA_VW, :])
          + _dot(od_ref[...], wo[GLA_VW:2 * GLA_VW, :])
          + _dot(oc_ref[...], wo[2 * GLA_VW:, :]))
    hx = _rms(x1, nx_ref[...]).astype(BF16)
    q = _dot(hx, wcq_ref[...])
    ss = _dot_exact_rhs(q * q, e_ref[...], 2) * (1.0 / X_DH)
    q = (q * lax.rsqrt(ss + NORM_EPS) * qn_ref[...]).astype(BF16)
    logits = _dot_nt(q, kbd_ref[...])
    ps = []
    for hh in range(X_HEADS):
        lg = logits[:, hh * mlen:(hh + 1) * mlen]
        ps.append(jnp.exp(lg - jnp.max(lg, axis=-1, keepdims=True)))
    p = jnp.concatenate(ps, axis=-1).astype(BF16)
    o = _dot(p, vbd_ref[...]) / _dot(p, obd_ref[...])
    x2 = x1 + _dot(o.astype(BF16), wco_ref[...])
    x2_ref[...] = x2
    hf = _rms(x2, nf_ref[...])
    hf_ref[...] = hf
    rl = lax.dot_general(wr_ref[...], hf, (((1,), (1,)), ((), ())), preferred_element_type=F32,
                         precision=lax.Precision.HIGHEST) + br_ref[...]
    row = lambda r: rl[r:r + 1, :]
    gl = [row(N_EXPERTS + g) for g in range(N_GROUPS)]
    g_max = functools.reduce(jnp.maximum, gl)
    g_sel = _first_index(gl, g_max)
    g_w = 1.0 / functools.reduce(lambda a, b: a + b, [jnp.exp(g - g_max) for g in gl])
    el = [_select_by_index(g_sel, [row(g * EXPERTS_PER_GROUP + j) for g in range(N_GROUPS)])
          for j in range(EXPERTS_PER_GROUP)]
    e1 = functools.reduce(jnp.maximum, el)
    i1 = _first_index(el, e1)
    rest = [jnp.where(i1 == j, NEG, el[j]) for j in range(EXPERTS_PER_GROUP)]
    e2 = functools.reduce(jnp.maximum, rest)
    i2 = _first_index(rest, e2)
    p2 = jnp.exp(e2 - e1)
    w1 = g_w / (1.0 + p2)
    w2 = g_w * p2 / (1.0 + p2)
    a = jnp.minimum(i1, i2)
    b = jnp.maximum(i1, i2)
    pair = jnp.where(a == 0, 0, jnp.where(a == 1, 3, 5)) + b - a - 1
    bucket = g_sel * MOE_PAIRS + pair
    zero = jnp.zeros_like(w1)
    route_ref[...] = jnp.concatenate(
        [bucket.astype(F32), jnp.where(i1 < i2, w1, w2), jnp.where(i1 < i2, w2, w1)] + [zero] * 5,
        axis=0)


def _first_index(rows, value):
    idx = jnp.full(value.shape, len(rows) - 1, I32)
    for j in range(len(rows) - 2, -1, -1):
        idx = jnp.where(rows[j] == value, j, idx)
    return idx


def _select_by_index(idx, rows):
    out = rows[-1]
    for j in range(len(rows) - 2, -1, -1):
        out = jnp.where(idx == j, rows[j], out)
    return out


def _post(x2d, o_gla, o_dsa, o_conv, w_out, nx, w_cq, qn_t, e_x, kbd, vbd, obd, w_co, nf, w_r, b_r,
          B, S, M, tm):
    T, D = x2d.shape
    nblk = S // tm
    const = lambda i: (0, 0)
    row = lambda i: (i, 0)
    bat = lambda i: (i // nblk, 0)
    HM = X_HEADS * M
    return pl.pallas_call(
        functools.partial(_post_kernel, mlen=M),
        out_shape=(jax.ShapeDtypeStruct((T, D), F32), jax.ShapeDtypeStruct((T, D), F32),
                   jax.ShapeDtypeStruct((8, T), F32)),
        grid=(T // tm,),
        in_specs=[
            pl.BlockSpec((tm, D), row),
            pl.BlockSpec((tm, GLA_VW), row),
            pl.BlockSpec((tm, DSA_HEADS * DSA_DH), row),
            pl.BlockSpec((tm, CONV_CH), row),
            pl.BlockSpec((D, D), const),
            pl.BlockSpec((1, D), const),
            pl.BlockSpec((D, XW), const),
            pl.BlockSpec((1, XW), const),
            pl.BlockSpec((XW, XW), const),
            pl.BlockSpec((HM, XW), bat),
            pl.BlockSpec((HM, XW), bat),
            pl.BlockSpec((HM, XW), const),
            pl.BlockSpec((XW, D), const),
            pl.BlockSpec((1, D), const),
            pl.BlockSpec((LANES, D), const),
            pl.BlockSpec((LANES, 1), const),
        ],
        out_specs=(pl.BlockSpec((tm, D), row), pl.BlockSpec((tm, D), row),
                   pl.BlockSpec((8, tm), lambda i: (0, i))),
        compiler_params=_cparams(("arbitrary",)),
        name="post",
    )(x2d, o_gla, o_dsa, o_conv, w_out, nx, w_cq, qn_t, e_x, kbd, vbd, obd, w_co, nf, w_r, b_r)


MOE_TR = 256
MOE_RANK_TL = 512
MOE_DMA_WINDOW = 64
BUCKET_ROWS = 32


def _moe_rank_kernel(route_ref, triu_ref, rank_ref, count_ref, carry_ref):
    @pl.when(pl.program_id(0) == 0)
    def _():
        carry_ref[...] = jnp.zeros_like(carry_ref)

    bucket = route_ref[0:1, :].astype(I32)
    onehot = (lax.broadcasted_iota(I32, (BUCKET_ROWS, MOE_RANK_TL), 0) == bucket)
    incl = _dot(onehot.astype(BF16), triu_ref[...])
    carry = carry_ref[...]
    before = jnp.where(onehot, incl - 1.0 + carry[:, 0:1], 0.0)
    rank_ref[...] = jnp.sum(before, axis=0, keepdims=True).astype(I32)
    carry = carry + incl[:, MOE_RANK_TL - 1:MOE_RANK_TL]
    carry_ref[...] = carry
    count_ref[...] = carry


def _moe_rank(route):
    T = route.shape[1]
    tl = MOE_RANK_TL
    triu = jnp.asarray(np.triu(np.ones((tl, tl), np.float32)), BF16)
    return pl.pallas_call(
        _moe_rank_kernel,
        out_shape=(jax.ShapeDtypeStruct((1, T), I32), jax.ShapeDtypeStruct((BUCKET_ROWS, LANES), F32)),
        grid=(T // tl,),
        in_specs=[pl.BlockSpec((8, tl), lambda i: (0, i)), pl.BlockSpec((tl, tl), lambda i: (0, 0))],
        out_specs=(pl.BlockSpec((1, tl), lambda i: (0, i)),
                   pl.BlockSpec((BUCKET_ROWS, LANES), lambda i: (0, 0))),
        scratch_shapes=[pltpu.VMEM((BUCKET_ROWS, LANES), F32)],
        compiler_params=_cparams(("arbitrary",)),
        name="moe_rank",
    )(route, triu)


def _moe_gather_kernel(src_ref, x_ref, o_ref, sem, *, nrows):
    win = MOE_DMA_WINDOW

    def start_window(w):
        def body(r, carry):
            row = w * win + r
            pltpu.make_async_copy(x_ref.at[pl.ds(src_ref[row], 1)], o_ref.at[pl.ds(row, 1)], sem).start()
            return carry

        lax.fori_loop(0, win, body, 0, unroll=8)

    def wait_window(w):
        rows = pl.ds(w * win, win)
        pltpu.make_async_copy(o_ref.at[rows], o_ref.at[rows], sem).wait()

    start_window(0)

    def body(w, carry):
        start_window(w)
        wait_window(w - 1)
        return carry

    nwin = nrows // win
    lax.fori_loop(1, nwin, body, 0)
    wait_window(nwin - 1)


def _moe_gather(src, x):
    nrows = src.shape[0]
    return pl.pallas_call(
        functools.partial(_moe_gather_kernel, nrows=nrows),
        out_shape=jax.ShapeDtypeStruct((nrows, x.shape[1]), x.dtype),
        grid_spec=pltpu.PrefetchScalarGridSpec(
            num_scalar_prefetch=1, grid=(1,),
            in_specs=[pl.BlockSpec(memory_space=pl.ANY)],
            out_specs=pl.BlockSpec(memory_space=pl.ANY),
            scratch_shapes=[pltpu.SemaphoreType.DMA(())]),
        compiler_params=pltpu.CompilerParams(dimension_semantics=("arbitrary",),
                                             has_side_effects=True),
        name="moe_gather",
    )(src, x)


def _moe_experts_kernel(elo_ref, ehi_ref, used_ref, xs_ref, g_ref, wgl_ref, wul_ref, wdl_ref,
                        wgh_ref, wuh_ref, wdh_ref, o_ref):
    used = used_ref[pl.program_id(0)] > 0

    @pl.when(jnp.logical_not(used))
    def _():
        o_ref[...] = jnp.zeros_like(o_ref)

    @pl.when(used)
    def _():
        t = xs_ref[...].astype(BF16)
        g = g_ref[0].T
        y = None
        for col, (wg, wu, wd) in enumerate(((wgl_ref, wul_ref, wdl_ref), (wgh_ref, wuh_ref, wdh_ref))):
            a = _dot(t, wg[0])
            u = _dot(t, wu[0])
            he = (a / (1.0 + jnp.exp(-a))) * u * g[:, col:col + 1]
            part = _dot(he.astype(BF16), wd[0])
            y = part if y is None else y + part
        o_ref[...] = y


def _moe_experts(e_lo, e_hi, used, xs, gates, wg, wu, wd):
    nrows, D = xs.shape
    DE = wg.shape[-1]
    tr = MOE_TR
    lo = lambda j, el, eh, us: (el[j], 0, 0)
    hi = lambda j, el, eh, us: (eh[j], 0, 0)
    return pl.pallas_call(
        _moe_experts_kernel,
        out_shape=jax.ShapeDtypeStruct((nrows, D), F32),
        grid_spec=pltpu.PrefetchScalarGridSpec(
            num_scalar_prefetch=3, grid=(nrows // tr,),
            in_specs=[
                pl.BlockSpec((tr, D), lambda j, el, eh, us: (j, 0)),
                pl.BlockSpec((1, 8, tr), lambda j, el, eh, us: (j, 0, 0)),
                pl.BlockSpec((1, D, DE), lo), pl.BlockSpec((1, D, DE), lo), pl.BlockSpec((1, DE, D), lo),
                pl.BlockSpec((1, D, DE), hi), pl.BlockSpec((1, D, DE), hi), pl.BlockSpec((1, DE, D), hi),
            ],
            out_specs=pl.BlockSpec((tr, D), lambda j, el, eh, us: (j, 0))),
        compiler_params=_cparams(("arbitrary",)),
        name="moe_experts",
    )(e_lo, e_hi, used, xs, gates, wg, wu, wd, wg, wu, wd)


def _moe_combine_kernel(slot_ref, x2_ref, ys_ref, o_ref, buf_ref, sem, *, tm, ntile):
    i = pl.program_id(0)

    def fetch(tile, slot, start):
        if not start:
            pltpu.make_async_copy(ys_ref.at[pl.ds(0, tm)], buf_ref.at[slot], sem.at[slot]).wait()
            return

        def body(r, carry):
            pltpu.make_async_copy(ys_ref.at[pl.ds(slot_ref[tile * tm + r], 1)],
                                  buf_ref.at[slot, pl.ds(r, 1)], sem.at[slot]).start()
            return carry

        lax.fori_loop(0, tm, body, 0, unroll=8)

    @pl.when(i == 0)
    def _():
        fetch(0, 0, True)

    @pl.when(i + 1 < ntile)
    def _():
        fetch(i + 1, (i + 1) % 2, True)

    fetch(i, i % 2, False)
    o_ref[...] = x2_ref[...] + buf_ref[i % 2]


def _moe_combine(slot, x2, ys, tm):
    T, D = x2.shape
    ntile = T // tm
    return pl.pallas_call(
        functools.partial(_moe_combine_kernel, tm=tm, ntile=ntile),
        out_shape=jax.ShapeDtypeStruct((T, D), F32),
        grid_spec=pltpu.PrefetchScalarGridSpec(
            num_scalar_prefetch=1, grid=(ntile,),
            in_specs=[pl.BlockSpec((tm, D), lambda i, sl: (i, 0)),
                      pl.BlockSpec(memory_space=pl.ANY)],
            out_specs=pl.BlockSpec((tm, D), lambda i, sl: (i, 0)),
            scratch_shapes=[pltpu.VMEM((2, tm, D), F32), pltpu.SemaphoreType.DMA((2,))]),
        compiler_params=_cparams(("arbitrary",)),
        name="moe_combine",
    )(slot, x2, ys)


def _moe(x2, hf, route, wg, wu, wd):
    T, D = x2.shape
    tr = MOE_TR
    ntile = T // tr + MOE_BUCKETS
    bucket = route[0].astype(I32)
    rank, counts = _moe_rank(route)
    counts = counts[:MOE_BUCKETS, 0].astype(I32)
    ntl = (counts + tr - 1) // tr
    tile_end = jnp.cumsum(ntl)
    start = (tile_end - ntl) * tr
    slot = start[bucket] + rank[0]
    src = jnp.zeros((ntile * tr,), I32).at[slot].set(jnp.arange(T, dtype=I32))
    tile_bucket = jnp.minimum(jnp.searchsorted(tile_end, jnp.arange(ntile, dtype=I32), side="right"),
                              MOE_BUCKETS - 1).astype(I32)
    used = (jnp.arange(ntile, dtype=I32) < tile_end[-1]).astype(I32)
    pair_lo = jnp.asarray([0, 0, 0, 1, 1, 2], I32)
    pair_hi = jnp.asarray([1, 2, 3, 2, 3, 3], I32)
    grp = tile_bucket // MOE_PAIRS
    e_lo = grp * EXPERTS_PER_GROUP + pair_lo[tile_bucket % MOE_PAIRS]
    e_hi = grp * EXPERTS_PER_GROUP + pair_hi[tile_bucket % MOE_PAIRS]
    gates = jnp.pad(route[1:3][:, src], ((0, 6), (0, 0)))
    gates = gates.reshape(8, ntile, tr).transpose(1, 0, 2)
    xs = _moe_gather(src, hf)
    ys = _moe_experts(e_lo, e_hi, used, xs, gates, wg, wu, wd)
    return _moe_combine(slot, x2, ys, tm=tr)


def _pad_heads(w, heads, dh, slot):
    lead = w.shape[:-1]
    w = w.reshape(lead + (heads, dh))
    w = jnp.pad(w, [(0, 0)] * len(lead) + [(0, 0), (0, slot - dh)])
    return w.reshape(lead + (heads * slot,))


def _pad_cols(w, width):
    return jnp.pad(w, [(0, 0)] * (w.ndim - 1) + [(0, width - w.shape[-1])])


def _layout_w_in(w):
    sizes = (GLA_HEADS * GLA_DK, GLA_HEADS * GLA_DK, GLA_VW, GLA_RANK, GLA_VW,
             DSA_HEADS * DSA_DH, DSA_HEADS * DSA_DH, DSA_HEADS * DSA_DH, IQ_W, IDX_DIM, IDX_HEADS,
             CU_W)
    offs = np.cumsum((0,) + sizes)
    g_q, g_k, g_v, g_lr, g_r, d_q, d_k, d_v, i_q, i_k, i_w, c_u = [
        w[:, offs[n]:offs[n + 1]] for n in range(len(sizes))]
    cols = [
        _pad_heads(d_q, DSA_HEADS, DSA_DH, LANES),
        _pad_heads(d_k, DSA_HEADS, DSA_DH, LANES),
        _pad_heads(d_v, DSA_HEADS, DSA_DH, LANES),
        g_v, g_r,
        _pad_heads(g_q, GLA_HEADS, GLA_DK, GLA_DKP),
        _pad_heads(g_k, GLA_HEADS, GLA_DK, GLA_DKP),
        i_q, c_u,
        _pad_cols(g_lr, LANES),
        _pad_cols(jnp.concatenate([i_k, i_w], axis=-1), LANES),
    ]
    out = jnp.concatenate(cols, axis=-1)
    assert out.shape[-1] == NP_IN
    return out.astype(BF16)


def _block_ones(n, blk):
    idx = np.arange(n) // blk
    return jnp.asarray((idx[:, None] == idx[None, :]).astype(np.float32))


def _layer(x2d, mem2d, rel_bias, p, B, S, M):
    D = x2d.shape[1]
    row = lambda v: v.reshape(1, -1)
    ones_dv = np.zeros((1, DSA_W), np.float32)
    ones_dv[0, DSA_DH::LANES] = 1.0
    z = _in_proj(x2d, row(p["norm_mix"]), _layout_w_in(p["w_in"]),
                 _pad_cols(row(p["dsa_qnorm"]), LANES), _pad_cols(row(p["dsa_knorm"]), LANES),
                 jnp.asarray(ones_dv), tm=min(512, S))
    wa2_p = jnp.pad(_pad_heads(p["gla_wa2"], GLA_HEADS, GLA_DK, GLA_DKP),
                    ((0, LANES - GLA_RANK), (0, 0)))
    wa2_hi = wa2_p.astype(BF16)
    wa2_p = jnp.stack([wa2_hi, (wa2_p - wa2_hi.astype(F32)).astype(BF16)])
    o_gla = _gla(z, wa2_p, row(_pad_heads(p["gla_ba"], GLA_HEADS, GLA_DK, GLA_DKP)),
                 row(jnp.tile(p["gla_onorm"], GLA_HEADS)), B, S, rows=min(256, S))
    o_dsa = _dsa(z, rel_bias, B, S)
    o_conv = _conv(z, p["conv_w"], row(p["conv_b"]), row(p["conv_ln_g"]), row(p["conv_ln_b"]),
                   B, S, tm=min(512, S))
    e_x = _block_ones(XW, X_DH).astype(BF16)
    kbd, vbd = _mem_kv(mem2d, row(p["norm_mem"]), p["w_ckv"].astype(BF16),
                       row(jnp.tile(p["x_knorm"], X_HEADS)), e_x, B, M)
    obd = (jnp.arange(X_HEADS * M)[:, None] // M == jnp.arange(XW)[None, :] // X_DH).astype(BF16)
    w_r = _pad_cols(jnp.concatenate([p["w_re"], p["w_rg"]], axis=-1), LANES).T
    b_r = _pad_cols(row(jnp.concatenate([p["b_re"], p["b_rg"]])), LANES).T
    x2, hf, route = _post(
        x2d, o_gla, o_dsa, o_conv, p["w_out"].astype(BF16), row(p["norm_x"]),
        p["w_cq"].astype(BF16), row(jnp.tile(p["x_qnorm"], X_HEADS)), e_x, kbd, vbd, obd,
        p["w_co"].astype(BF16), row(p["norm_ffn"]), w_r, b_r, B, S, M, tm=min(512, S))
    return _moe(x2, hf, route, p["w_gate"].astype(BF16), p["w_up"].astype(BF16),
                p["w_down"].astype(BF16))


_LAYER_KEYS = ("norm_mix", "w_in", "gla_wa2", "gla_ba", "gla_onorm", "dsa_qnorm", "dsa_knorm",
               "conv_w", "conv_b", "conv_ln_g", "conv_ln_b", "w_out", "norm_x", "norm_mem", "w_cq",
               "w_ckv", "x_qnorm", "x_knorm", "w_co", "norm_ffn", "w_rg", "b_rg", "w_re", "b_re",
               "w_gate", "w_up", "w_down")


def kernel(x, mem, rel_bias, norm_mix, w_in, gla_wa2, gla_ba, gla_onorm, dsa_qnorm, dsa_knorm,
           conv_w, conv_b, conv_ln_g, conv_ln_b, w_out, norm_x, norm_mem, w_cq, w_ckv, x_qnorm,
           x_knorm, w_co, norm_ffn, w_rg, b_rg, w_re, b_re, w_gate, w_up, w_down):
    stacked = dict(zip(_LAYER_KEYS, (
        norm_mix, w_in, gla_wa2, gla_ba, gla_onorm, dsa_qnorm, dsa_knorm, conv_w, conv_b,
        conv_ln_g, conv_ln_b, w_out, norm_x, norm_mem, w_cq, w_ckv, x_qnorm, x_knorm, w_co,
        norm_ffn, w_rg, b_rg, w_re, b_re, w_gate, w_up, w_down)))
    B, S, D = x.shape
    M = mem.shape[1]
    x2d = x.reshape(B * S, D)
    mem2d = mem.reshape(B * M, D)
    for l in range(norm_mix.shape[0]):
        x2d = _layer(x2d, mem2d, rel_bias, {k: v[l] for k, v in stacked.items()}, B, S, M)
    return x2d.reshape(B, S, D)
```

```python
import functools
import math

import jax
import jax.numpy as jnp
import numpy as np
from jax import lax
from jax.experimental import pallas as pl
from jax.experimental.pallas import tpu as pltpu

F32 = jnp.float32
BF16 = jnp.bfloat16
I32 = jnp.int32

LANES = 128
NORM_EPS = 1e-6

GLA_HEADS = 4
GLA_DK = 48
GLA_DV = 96
GLA_DKP = 64
GLA_RANK = 16
GLA_TAU = 16.0
DSA_HEADS = 6
DSA_DH = 64
IDX_HEADS = 8
IDX_DIM = 64
TOPK_MAX = 256
CONV_CH = 256
CONV_WIDTH = 31
NUM_BUCKETS = 32
MAX_DISTANCE = 128
X_HEADS = 4
X_DH = 64
N_GROUPS = 4
EXPERTS_PER_GROUP = 4
N_EXPERTS = 16
MOE_PAIRS = 6
MOE_BUCKETS = N_GROUPS * MOE_PAIRS

DSA_W = DSA_HEADS * LANES
GLA_VW = GLA_HEADS * GLA_DV
GLA_KW = GLA_HEADS * GLA_DKP
IQ_W = IDX_HEADS * IDX_DIM
CU_W = 2 * CONV_CH
OFF_DQ, OFF_DK, OFF_DV = 0, DSA_W, 2 * DSA_W
OFF_GV = 3 * DSA_W
OFF_GR = OFF_GV + GLA_VW
OFF_GQ = OFF_GR + GLA_VW
OFF_GK = OFF_GQ + GLA_KW
OFF_IQ = OFF_GK + GLA_KW
OFF_CU = OFF_IQ + IQ_W
OFF_LR = OFF_CU + CU_W
OFF_IK = OFF_LR + LANES
NP_IN = OFF_IK + LANES

NEG = -1e30
LOG2E = math.log2(math.e)
INT_MIN = -2 ** 31
INT_MAX = 2 ** 31 - 1

VMEM_LIMIT = 56 * 1024 * 1024


def _cparams(sem):
    return pltpu.CompilerParams(dimension_semantics=sem, vmem_limit_bytes=VMEM_LIMIT)


def _dot(a, b):
    return jnp.dot(a, b, preferred_element_type=F32)


def _dot_nt(a, b):
    return lax.dot_general(a, b, (((1,), (1,)), ((), ())), preferred_element_type=F32)


def _split_bf16(x, terms):
    parts = []
    for _ in range(terms):
        hi = x.astype(BF16)
        parts.append(hi)
        x = x - hi.astype(F32)
    return parts


def _dot_exact_lhs(a_bf16, b, terms):
    return functools.reduce(lambda u, v: u + v, [_dot(a_bf16, piece) for piece in _split_bf16(b, terms)])


def _dot_exact_rhs(a, b_bf16, terms):
    return functools.reduce(lambda u, v: u + v, [_dot(piece, b_bf16) for piece in _split_bf16(a, terms)])


def _rms(x, g):
    return x * lax.rsqrt(jnp.mean(x * x, axis=-1, keepdims=True) + NORM_EPS) * g


_IN_CHUNKS = ((OFF_DQ, DSA_W), (OFF_DK, DSA_W), (OFF_DV, DSA_W), (OFF_GV, 2 * GLA_VW),
              (OFF_GQ, 2 * GLA_KW), (OFF_IQ, IQ_W), (OFF_CU, CU_W), (OFF_LR, 2 * LANES))


def _in_proj_kernel(x_ref, g_ref, w_ref, qn_ref, kn_ref, ones_ref, o_ref):
    h = _rms(x_ref[...], g_ref[...]).astype(BF16)

    def head_norm(y, gain, scale):
        parts = []
        for hh in range(DSA_HEADS):
            yh = y[:, hh * LANES:(hh + 1) * LANES]
            ss = jnp.sum(yh * yh, axis=-1, keepdims=True) * (1.0 / DSA_DH)
            parts.append(yh * lax.rsqrt(ss + NORM_EPS) * gain * scale)
        return jnp.concatenate(parts, axis=-1)

    for c0, cw in _IN_CHUNKS:
        y = _dot(h, w_ref[:, c0:c0 + cw])
        if c0 == OFF_DQ:
            y = head_norm(y, qn_ref[...], DSA_DH ** -0.5 * LOG2E)
        elif c0 == OFF_DK:
            y = head_norm(y, kn_ref[...], 1.0)
        elif c0 == OFF_DV:
            y = y + ones_ref[...]
        o_ref[:, c0:c0 + cw] = y.astype(BF16)


def _in_proj(x2d, g, w_p, qn_p, kn_p, ones_p, tm):
    T, D = x2d.shape
    return pl.pallas_call(
        _in_proj_kernel,
        out_shape=jax.ShapeDtypeStruct((T, NP_IN), BF16),
        grid=(T // tm,),
        in_specs=[
            pl.BlockSpec((tm, D), lambda i: (i, 0)),
            pl.BlockSpec((1, D), lambda i: (0, 0)),
            pl.BlockSpec((D, NP_IN), lambda i: (0, 0)),
            pl.BlockSpec((1, LANES), lambda i: (0, 0)),
            pl.BlockSpec((1, LANES), lambda i: (0, 0)),
            pl.BlockSpec((1, DSA_W), lambda i: (0, 0)),
        ],
        out_specs=pl.BlockSpec((tm, NP_IN), lambda i: (i, 0)),
        compiler_params=_cparams(("arbitrary",)),
        name="in_proj",
    )(x2d, g, w_p, qn_p, kn_p, ones_p)


GLA_C = 64
GLA_SUB = 16
GLA_NSUB = GLA_C // GLA_SUB


def _gla_kernel(q_ref, k_ref, v_ref, r_ref, lr_ref, wa2_ref, ba_ref, on_ref, e_ref, e2_ref,
                et_ref, tril_ref, o_ref, st_ref, qf_ref, kf_ref, vf_ref, bf_ref, of_ref, *, rows):
    c = pl.program_id(1)

    @pl.when(c == 0)
    def _():
        st_ref[...] = jnp.zeros_like(st_ref)

    lr = lr_ref[...]
    pre = _dot(lr, wa2_ref[0]) + _dot(lr, wa2_ref[1]) + ba_ref[...]
    log_a = (jnp.minimum(pre, 0.0) - jnp.log(1.0 + jnp.exp(-jnp.abs(pre)))) * (LOG2E / GLA_TAU)
    bf_ref[...] = _dot_exact_lhs(tril_ref[...], log_a, 3)
    qf_ref[...] = q_ref[...].astype(F32) * (GLA_DK ** -0.5)
    kf_ref[...] = k_ref[...].astype(F32)
    vf_ref[...] = v_ref[...].astype(F32)
    e_mat = e_ref[...]
    row_id = lax.broadcasted_iota(I32, (GLA_SUB, GLA_KW), 0)

    def chunk(ci, carry):
        r0 = pl.multiple_of(ci * GLA_C, GLA_C)
        q = qf_ref[pl.ds(r0, GLA_C), :]
        k = kf_ref[pl.ds(r0, GLA_C), :]
        v = vf_ref[pl.ds(r0, GLA_C), :]
        b = bf_ref[pl.ds(r0, GLA_C), :]
        st = st_ref[...]
        o_inter = _dot_nt((q * jnp.exp2(b)).astype(BF16), st.astype(BF16))
        b_last = b[GLA_C - 1:GLA_C, :]
        kd = (k * jnp.exp2(b_last - b)).astype(BF16)
        upd = _dot(v.T.astype(BF16), kd) * et_ref[...]
        st_ref[...] = st * jnp.exp2(b_last) + upd
        outs = []
        for i in range(GLA_NSUB):
            qi = q[i * GLA_SUB:(i + 1) * GLA_SUB, :]
            bi = b[i * GLA_SUB:(i + 1) * GLA_SUB, :]
            acc = o_inter[i * GLA_SUB:(i + 1) * GLA_SUB, :]
            for j in range(i + 1):
                slabs = []
                for s in range(GLA_SUB):
                    row = r0 + (j * GLA_SUB + s)
                    ks = kf_ref[pl.ds(row, 1), :]
                    bs = bf_ref[pl.ds(row, 1), :]
                    if i == j:
                        p = jnp.where(row_id >= s, qi * ks * jnp.exp2(jnp.minimum(bi - bs, 0.0)), 0.0)
                    else:
                        p = qi * ks * jnp.exp2(bi - bs)
                    slabs.append(p.astype(BF16))
                pr = _dot(jnp.concatenate(slabs, axis=0), e_mat)
                for s in range(GLA_SUB):
                    vs = vf_ref[pl.ds(r0 + (j * GLA_SUB + s), 1), :]
                    acc = acc + pr[s * GLA_SUB:(s + 1) * GLA_SUB, :] * vs
            outs.append(acc)
        of_ref[pl.ds(r0, GLA_C), :] = jnp.concatenate(outs, axis=0)
        return carry

    lax.fori_loop(0, rows // GLA_C, chunk, 0)

    o = of_ref[...]
    ss = _dot_exact_rhs(o * o, e2_ref[...], 2) * (1.0 / GLA_DV)
    o = o * lax.rsqrt(ss + NORM_EPS) * on_ref[...]
    r = r_ref[...].astype(F32)
    o_ref[...] = (o * (r / (1.0 + jnp.exp(-r)))).astype(BF16)


def _gla(z, wa2_p, ba_p, on_p, B, S, rows):
    T = z.shape[0]
    nblk = S // rows
    e = np.zeros((GLA_KW, GLA_VW), np.float32)
    for h in range(GLA_HEADS):
        e[h * GLA_DKP:(h + 1) * GLA_DKP, h * GLA_DV:(h + 1) * GLA_DV] = 1.0
    e2 = np.zeros((GLA_VW, GLA_VW), np.float32)
    for h in range(GLA_HEADS):
        e2[h * GLA_DV:(h + 1) * GLA_DV, h * GLA_DV:(h + 1) * GLA_DV] = 1.0
    tr = np.zeros((rows, rows), np.float32)
    for cc in range(rows // GLA_C):
        tr[cc * GLA_C:(cc + 1) * GLA_C, cc * GLA_C:(cc + 1) * GLA_C] = np.tril(np.ones((GLA_C, GLA_C)))
    rowmap = lambda b, c: (b * nblk + c)
    const = lambda b, c: (0, 0)
    return pl.pallas_call(
        functools.partial(_gla_kernel, rows=rows),
        out_shape=jax.ShapeDtypeStruct((T, GLA_VW), BF16),
        grid=(B, nblk),
        in_specs=[
            pl.BlockSpec((rows, GLA_KW), lambda b, c: (rowmap(b, c), OFF_GQ // GLA_KW)),
            pl.BlockSpec((rows, GLA_KW), lambda b, c: (rowmap(b, c), OFF_GK // GLA_KW)),
            pl.BlockSpec((rows, GLA_VW), lambda b, c: (rowmap(b, c), OFF_GV // GLA_VW)),
            pl.BlockSpec((rows, GLA_VW), lambda b, c: (rowmap(b, c), OFF_GR // GLA_VW)),
            pl.BlockSpec((rows, LANES), lambda b, c: (rowmap(b, c), OFF_LR // LANES)),
            pl.BlockSpec((2, LANES, GLA_KW), lambda b, c: (0, 0, 0)),
            pl.BlockSpec((1, GLA_KW), const),
            pl.BlockSpec((1, GLA_VW), const),
            pl.BlockSpec((GLA_KW, GLA_VW), const),
            pl.BlockSpec((GLA_VW, GLA_VW), const),
            pl.BlockSpec((GLA_VW, GLA_KW), const),
            pl.BlockSpec((rows, rows), const),
        ],
        out_specs=pl.BlockSpec((rows, GLA_VW), lambda b, c: (rowmap(b, c), 0)),
        scratch_shapes=[
            pltpu.VMEM((GLA_VW, GLA_KW), F32),
            pltpu.VMEM((rows, GLA_KW), F32),
            pltpu.VMEM((rows, GLA_KW), F32),
            pltpu.VMEM((rows, GLA_VW), F32),
            pltpu.VMEM((rows, GLA_KW), F32),
            pltpu.VMEM((rows, GLA_VW), F32),
        ],
        compiler_params=_cparams(("arbitrary", "arbitrary")),
        name="gla",
    )(z, z, z, z, z, wa2_p, ba_p, on_p, jnp.asarray(e, BF16), jnp.asarray(e2, BF16), jnp.asarray(e.T),
      jnp.asarray(tr, BF16))


DSA_TQ = 256


def _t5_bucket(dist):
    max_exact = NUM_BUCKETS // 2
    d = jnp.maximum(dist, 1).astype(F32)
    large = max_exact + (jnp.log(d / max_exact) / math.log(MAX_DISTANCE / max_exact)
                         * (NUM_BUCKETS - max_exact)).astype(I32)
    large = jnp.minimum(large, NUM_BUCKETS - 1)
    return jnp.where(dist < max_exact, dist, large)


def _block_loop(start, stop, body, init):
    npair = (stop - start) // 2
    carry = lax.fori_loop(0, npair, lambda i, c: body([start + 2 * i, start + 2 * i + 1], c), init)
    return lax.cond((stop - start) % 2 == 1, lambda c: body([stop - 1], c), lambda c: c, carry)


def _dsa_kernel(rb_ref, dq_ref, dk_ref, dv_ref, iq_ref, ik_ref, iw_ref, o_ref,
                bias_ref, key_ref, key16_ref, vt_ref, iqs_ref, lo_ref, hi_ref, clo_ref, chi_ref,
                m_ref, acc_ref, lga_ref, lgb_ref, *, seq, topk):
    TQ = DSA_TQ
    bi = pl.program_id(0)
    qb = pl.program_id(1)
    nkb = qb + 1
    log2_seq = int(math.log2(seq))

    s_row = lax.broadcasted_iota(I32, (TQ, TQ), 0)
    t_lane = lax.broadcasted_iota(I32, (TQ, TQ), 1)

    @pl.when((bi == 0) & (qb == 0))
    def _():
        for delta in range(2):
            bucket = _t5_bucket(jnp.maximum(delta * TQ + t_lane - s_row, 0))
            for hh in range(DSA_HEADS):
                tile = jnp.zeros((TQ, TQ), F32)
                for bk in range(NUM_BUCKETS):
                    tile = jnp.where(bucket == bk, rb_ref[bk, hh] * LOG2E, tile)
                bias_ref[hh, delta] = tile

    @pl.when(qb == 0)
    def _():
        def tr(kb, carry):
            k0 = pl.multiple_of(kb * TQ, TQ)
            for hh in range(DSA_HEADS):
                blk = dv_ref[pl.ds(k0, TQ), hh * LANES:(hh + 1) * LANES].astype(F32)
                vt_ref[kb, hh] = blk.T.astype(BF16)
            return carry

        lax.fori_loop(0, seq // TQ, tr, 0)

    for hh in range(IDX_HEADS):
        iqs_ref[hh * TQ:(hh + 1) * TQ, :] = iq_ref[:, hh * IDX_DIM:(hh + 1) * IDX_DIM]
    w_t = iw_ref[...].astype(F32).T * ((IDX_HEADS ** -0.5) * (IDX_DIM ** -0.5))

    def score_blocks(kbs, carry):
        k0s = [pl.multiple_of(kb * TQ, TQ) for kb in kbs]
        dots = [_dot_nt(ik_ref[pl.ds(k0, TQ), :][:, :IDX_DIM], iqs_ref[...]) for k0 in k0s]
        for kb, k0, dt in zip(kbs, k0s, dots):
            sc = jnp.zeros((TQ, TQ), F32)
            for hh in range(IDX_HEADS):
                sc = sc + (jnp.maximum(dt[:, hh * TQ:(hh + 1) * TQ], 0.0)
                           * w_t[IDX_DIM + hh:IDX_DIM + hh + 1, :])
            bits = pltpu.bitcast(sc, I32)
            keys = bits ^ ((bits >> 31) & INT_MAX)
            causal = (k0 + s_row) <= (qb * TQ + t_lane)
            keys = jnp.where(causal, keys, INT_MIN)
            key_ref[kb] = keys
            key16_ref[kb] = (keys >> 16).astype(jnp.int16)
        return carry

    _block_loop(0, nkb, score_blocks, 0)

    ncausal = qb * TQ + lax.broadcasted_iota(I32, (1, TQ), 1) + 1
    short = ncausal <= topk

    def count_cols(pred, dtype, ref):
        rows = 8 * (4 // jnp.dtype(dtype).itemsize)

        def body(kbs, cnt):
            for kb in kbs:
                hit = jnp.where(pred(ref[kb], kb), jnp.ones((), dtype), jnp.zeros((), dtype))
                for r in range(TQ // rows):
                    cnt = cnt + hit[r * rows:(r + 1) * rows, :]
            return cnt

        cnt = _block_loop(0, nkb, body, jnp.zeros((rows, TQ), dtype))
        return jnp.sum(cnt.astype(I32), axis=0, keepdims=True)

    def bisect_step(state, count_ge):
        lo, hi, clo, chi = state
        mid = (lo >> 1) + (hi >> 1) + (lo & hi & 1)
        c = count_ge(mid)
        up = (c >= topk) & (mid != lo)
        down = c < topk
        return (jnp.where(up, mid, lo), jnp.where(down, mid, hi),
                jnp.where(up, c, clo), jnp.where(down, c, chi))

    def step16(_, state):
        return bisect_step(state, lambda mid: count_cols(
            lambda kk, kb: kk >= mid.astype(jnp.int16), jnp.int16, key16_ref))

    lo16, _, clo, chi = lax.fori_loop(
        0, 16, step16,
        (jnp.full((1, TQ), -2 ** 15 + 1, I32), jnp.full((1, TQ), 2 ** 15, I32),
         ncausal, jnp.zeros((1, TQ), I32)))
    lo16_h = lo16.astype(jnp.int16)

    def low_half(kbs, carry):
        for kb in kbs:
            low = ((key_ref[kb] & 0xFFFF) - 2 ** 15).astype(jnp.int16)
            key16_ref[kb] = jnp.where(key16_ref[kb] == lo16_h, low, jnp.int16(-2 ** 15))
        return carry

    _block_loop(0, nkb, low_half, 0)
    lo_ref[...] = jnp.full((1, TQ), -2 ** 15, I32)
    hi_ref[...] = jnp.full((1, TQ), 2 ** 15, I32)
    clo_ref[...] = clo
    chi_ref[...] = chi
    above = chi

    def active_count():
        open_q = ((clo_ref[...] != topk) & (hi_ref[...] - lo_ref[...] > 1)
                  & jnp.logical_not(short))
        return jnp.sum(open_q.astype(I32))

    def bis_body(state):
        it, _ = state
        st = (lo_ref[...], hi_ref[...], clo_ref[...], chi_ref[...])
        for _ in range(2):
            frozen = st[2] == topk
            new = bisect_step(st, lambda mid: above + count_cols(
                lambda kk, kb: kk >= mid.astype(jnp.int16), jnp.int16, key16_ref))
            st = tuple(jnp.where(frozen, o, n) for o, n in zip(st, new))
        lo_ref[...], hi_ref[...], clo_ref[...], chi_ref[...] = st
        return it + 1, active_count()

    lax.while_loop(lambda s_: (s_[0] < 9) & (s_[1] > 0), bis_body, (jnp.int32(0), active_count()))
    lo_ref[...] = jnp.where(short, INT_MIN + 1, (lo16 << 16) + (lo_ref[...] + 2 ** 15))

    tie_q = (clo_ref[...] > topk) & jnp.logical_not(short)

    @pl.when(jnp.sum(tie_q.astype(I32)) > 0)
    def _():
        thr = lo_ref[...]
        need = topk - chi_ref[...]
        plo = jnp.full((1, TQ), -1, I32)
        phi = jnp.full((1, TQ), seq - 1, I32)
        for _ in range(log2_seq):
            pm = (plo + phi) >> 1
            c = count_cols(lambda kk, kb: (kk == thr) & ((kb * TQ + s_row) <= pm), I32, key_ref)
            ok = c >= need
            phi = jnp.where(ok, pm, phi)
            plo = jnp.where(ok, plo, pm)
        cut = jnp.where(tie_q, phi, seq - 1)

        def demote(kb, carry):
            kk = key_ref[kb]
            drop = (kk == thr) & ((kb * TQ + s_row) > cut)
            key_ref[kb] = jnp.where(drop, INT_MIN, kk)
            return carry

        lax.fori_loop(0, nkb, demote, 0)

    m_ref[...] = jnp.full(m_ref.shape, NEG, F32)
    acc_ref[...] = jnp.zeros_like(acc_ref)
    thr_sel = lo_ref[...]

    heads = range(DSA_HEADS)

    def qk(kb, hh):
        k0 = pl.multiple_of(kb * TQ, TQ)
        return _dot_nt(dk_ref[pl.ds(k0, TQ), hh * LANES:(hh + 1) * LANES],
                       dq_ref[:, hh * LANES:(hh + 1) * LANES])

    def softmax_update(kb, logits, shift):
        sel = key_ref[kb] >= thr_sel
        m_prev = [m_ref[hh] for hh in heads]
        acc_prev = [acc_ref[hh] for hh in heads]
        lg = [jnp.where(sel, logits[hh], NEG) for hh in heads]
        m_new = [jnp.maximum(m_prev[hh], jnp.max(lg[hh], axis=0, keepdims=True) + shift[hh])
                 for hh in heads]
        p = [jnp.exp2(lg[hh] - (m_new[hh] - shift[hh])).astype(BF16) for hh in heads]
        pv = [_dot(vt_ref[kb, hh], p[hh]) for hh in heads]
        for hh in heads:
            acc_ref[hh] = acc_prev[hh] * jnp.exp2(m_prev[hh] - m_new[hh]) + pv[hh]
            m_ref[hh] = m_new[hh]

    n_far = jnp.maximum(qb - 1, 0)
    far_shift = [rb_ref[NUM_BUCKETS - 1, hh] * LOG2E for hh in heads]

    @pl.when(n_far > 0)
    def _():
        for hh in heads:
            lga_ref[hh] = qk(0, hh)

    def far_pair(i, carry):
        ka = 2 * i
        for hh in heads:
            lgb_ref[hh] = qk(ka + 1, hh)
        softmax_update(ka, [lga_ref[hh] for hh in heads], far_shift)
        for hh in heads:
            lga_ref[hh] = qk(jnp.minimum(ka + 2, n_far - 1), hh)
        softmax_update(ka + 1, [lgb_ref[hh] for hh in heads], far_shift)
        return carry

    lax.fori_loop(0, n_far // 2, far_pair, 0)

    @pl.when(n_far % 2 == 1)
    def _():
        softmax_update(n_far - 1, [lga_ref[hh] for hh in heads], far_shift)

    def near_body(kb, carry):
        softmax_update(kb, [qk(kb, hh) + bias_ref[hh, qb - kb] for hh in heads], [0.0] * DSA_HEADS)
        return carry

    lax.fori_loop(n_far, nkb, near_body, 0)

    outs = []
    for hh in range(DSA_HEADS):
        a = acc_ref[hh]
        outs.append(a[:DSA_DH, :] / a[DSA_DH:DSA_DH + 1, :])
    o_ref[...] = jnp.concatenate(outs, axis=0).T.astype(BF16)


def _dsa(z, rel_bias, B, S):
    T = z.shape[0]
    TQ = DSA_TQ
    nqb = S // TQ
    topk = min(TOPK_MAX, S // 4)
    qrow = lambda b, q: b * nqb + q
    return pl.pallas_call(
        functools.partial(_dsa_kernel, seq=S, topk=topk),
        out_shape=jax.ShapeDtypeStruct((T, DSA_HEADS * DSA_DH), BF16),
        grid=(B, nqb),
        in_specs=[
            pl.BlockSpec(memory_space=pltpu.SMEM),
            pl.BlockSpec((TQ, DSA_W), lambda b, q: (qrow(b, q), OFF_DQ // DSA_W)),
            pl.BlockSpec((S, DSA_W), lambda b, q: (b, OFF_DK // DSA_W)),
            pl.BlockSpec((S, DSA_W), lambda b, q: (b, OFF_DV // DSA_W)),
            pl.BlockSpec((TQ, IQ_W), lambda b, q: (qrow(b, q), OFF_IQ // IQ_W)),
            pl.BlockSpec((S, LANES), lambda b, q: (b, OFF_IK // LANES)),
            pl.BlockSpec((TQ, LANES), lambda b, q: (qrow(b, q), OFF_IK // LANES)),
        ],
        out_specs=pl.BlockSpec((TQ, DSA_HEADS * DSA_DH), lambda b, q: (qrow(b, q), 0)),
        scratch_shapes=[
            pltpu.VMEM((DSA_HEADS, 2, TQ, TQ), F32),
            pltpu.VMEM((S // TQ, TQ, TQ), I32),
            pltpu.VMEM((S // TQ, TQ, TQ), jnp.int16),
            pltpu.VMEM((S // TQ, DSA_HEADS, LANES, TQ), BF16),
            pltpu.VMEM((IDX_HEADS * TQ, IDX_DIM), BF16),
            pltpu.VMEM((1, TQ), I32),
            pltpu.VMEM((1, TQ), I32),
            pltpu.VMEM((1, TQ), I32),
            pltpu.VMEM((1, TQ), I32),
            pltpu.VMEM((DSA_HEADS, 1, TQ), F32),
            pltpu.VMEM((DSA_HEADS, LANES, TQ), F32),
            pltpu.VMEM((DSA_HEADS, TQ, TQ), F32),
            pltpu.VMEM((DSA_HEADS, TQ, TQ), F32),
        ],
        compiler_params=_cparams(("arbitrary", "arbitrary")),
        name="dsa",
    )(rel_bias, z, z, z, z, z, z)


CONV_HALO = 32


def _conv_kernel(u_ref, w_ref, cb_ref, g_ref, b_ref, o_ref, h_ref, *, tm):
    @pl.when(pl.program_id(1) == 0)
    def _():
        h_ref[0:CONV_HALO, :] = jnp.zeros((CONV_HALO, CONV_CH), F32)

    u = u_ref[...].astype(F32)
    a = u[:, :CONV_CH]
    g = u[:, CONV_CH:]
    h_ref[CONV_HALO:CONV_HALO + tm, :] = a / (1.0 + jnp.exp(-g))
    acc = jnp.zeros((tm, CONV_CH), F32) + cb_ref[...]
    base = CONV_HALO - (CONV_WIDTH - 1)
    for j in range(CONV_WIDTH):
        acc = acc + h_ref[base + j:base + j + tm, :] * w_ref[j:j + 1, :]
    h_ref[0:CONV_HALO, :] = h_ref[tm:tm + CONV_HALO, :]
    mu = jnp.mean(acc, axis=-1, keepdims=True)
    d = acc - mu
    var = jnp.mean(d * d, axis=-1, keepdims=True)
    y = d * lax.rsqrt(var + NORM_EPS) * g_ref[...] + b_ref[...]
    o_ref[...] = (y / (1.0 + jnp.exp(-y))).astype(BF16)


def _conv(z, w, cb, g, b, B, S, tm):
    T = z.shape[0]
    nblk = S // tm
    const = lambda bb, c: (0, 0)
    return pl.pallas_call(
        functools.partial(_conv_kernel, tm=tm),
        out_shape=jax.ShapeDtypeStruct((T, CONV_CH), BF16),
        grid=(B, nblk),
        in_specs=[
            pl.BlockSpec((tm, CU_W), lambda bb, c: (bb * nblk + c, OFF_CU // CU_W)),
            pl.BlockSpec((CONV_WIDTH, CONV_CH), const),
            pl.BlockSpec((1, CONV_CH), const),
            pl.BlockSpec((1, CONV_CH), const),
            pl.BlockSpec((1, CONV_CH), const),
        ],
        out_specs=pl.BlockSpec((tm, CONV_CH), lambda bb, c: (bb * nblk + c, 0)),
        scratch_shapes=[pltpu.VMEM((tm + CONV_HALO, CONV_CH), F32)],
        compiler_params=_cparams(("arbitrary", "arbitrary")),
        name="conv",
    )(z, w, cb, g, b)


XW = X_HEADS * X_DH


def _mem_kv_kernel(m_ref, g_ref, w_ref, kn_ref, e_ref, k_ref, v_ref, *, mlen):
    hm = _rms(m_ref[...], g_ref[...]).astype(BF16)
    kv = _dot(hm, w_ref[...])
    k = kv[:, :XW]
    v = kv[:, XW:]
    ss = _dot_exact_rhs(k * k, e_ref[...], 2) * (1.0 / X_DH)
    k = k * lax.rsqrt(ss + NORM_EPS) * kn_ref[...] * (X_DH ** -0.5)
    lane_head = lax.broadcasted_iota(I32, (mlen, XW), 1) // X_DH
    for hh in range(X_HEADS):
        msk = lane_head == hh
        k_ref[hh * mlen:(hh + 1) * mlen, :] = jnp.where(msk, k, 0.0).astype(BF16)
        v_ref[hh * mlen:(hh + 1) * mlen, :] = jnp.where(msk, v, 0.0).astype(BF16)


def _mem_kv(mem2d, g, w_ckv, kn_t, e_x, B, M):
    D = mem2d.shape[1]
    const = lambda b: (0, 0)
    out = jax.ShapeDtypeStruct((B * X_HEADS * M, XW), BF16)
    return pl.pallas_call(
        functools.partial(_mem_kv_kernel, mlen=M),
        out_shape=(out, out),
        grid=(B,),
        in_specs=[
            pl.BlockSpec((M, D), lambda b: (b, 0)),
            pl.BlockSpec((1, D), const),
            pl.BlockSpec((D, 2 * XW), const),
            pl.BlockSpec((1, XW), const),
            pl.BlockSpec((XW, XW), const),
        ],
        out_specs=(pl.BlockSpec((X_HEADS * M, XW), lambda b: (b, 0)),
                   pl.BlockSpec((X_HEADS * M, XW), lambda b: (b, 0))),
        compiler_params=_cparams(("arbitrary",)),
        name="mem_kv",
    )(mem2d, g, w_ckv, kn_t, e_x)


def _post_kernel(x_ref, og_ref, od_ref, oc_ref, wo_ref, nx_ref, wcq_ref, qn_ref, e_ref,
                 kbd_ref, vbd_ref, obd_ref, wco_ref, nf_ref, wr_ref, br_ref,
                 x2_ref, hf_ref, route_ref, *, mlen):
    wo = wo_ref
    x1 = (x_ref[...]
          + _dot(og_ref[...], wo[0:GLA_VW, :])
          + _dot(od_ref[...], wo[GLA_VW:2 * GLA_VW, :])
          + _dot(oc_ref[...], wo[2 * GLA_VW:, :]))
    hx = _rms(x1, nx_ref[...]).astype(BF16)
    q = _dot(hx, wcq_ref[...])
    ss = _dot_exact_rhs(q * q, e_ref[...], 2) * (1.0 / X_DH)
    q = (q * lax.rsqrt(ss + NORM_EPS) * qn_ref[...]).astype(BF16)
    logits = _dot_nt(q, kbd_ref[...])
    ps = []
    for hh in range(X_HEADS):
        lg = logits[:, hh * mlen:(hh + 1) * mlen]
        ps.append(jnp.exp(lg - jnp.max(lg, axis=-1, keepdims=True)))
    p = jnp.concatenate(ps, axis=-1).astype(BF16)
    o = _dot(p, vbd_ref[...]) / _dot(p, obd_ref[...])
    x2 = x1 + _dot(o.astype(BF16), wco_ref[...])
    x2_ref[...] = x2
    hf = _rms(x2, nf_ref[...])
    hf_ref[...] = hf
    rl = lax.dot_general(wr_ref[...], hf, (((1,), (1,)), ((), ())), preferred_element_type=F32,
                         precision=lax.Precision.HIGHEST) + br_ref[...]
    row = lambda r: rl[r:r + 1, :]
    gl = [row(N_EXPERTS + g) for g in range(N_GROUPS)]
    g_max = functools.reduce(jnp.maximum, gl)
    g_sel = _first_index(gl, g_max)
    g_w = 1.0 / functools.reduce(lambda a, b: a + b, [jnp.exp(g - g_max) for g in gl])
    el = [_select_by_index(g_sel, [row(g * EXPERTS_PER_GROUP + j) for g in range(N_GROUPS)])
          for j in range(EXPERTS_PER_GROUP)]
    e1 = functools.reduce(jnp.maximum, el)
    i1 = _first_index(el, e1)
    rest = [jnp.where(i1 == j, NEG, el[j]) for j in range(EXPERTS_PER_GROUP)]
    e2 = functools.reduce(jnp.maximum, rest)
    i2 = _first_index(rest, e2)
    p2 = jnp.exp(e2 - e1)
    w1 = g_w / (1.0 + p2)
    w2 = g_w * p2 / (1.0 + p2)
    a = jnp.minimum(i1, i2)
    b = jnp.maximum(i1, i2)
    pair = jnp.where(a == 0, 0, jnp.where(a == 1, 3, 5)) + b - a - 1
    bucket = g_sel * MOE_PAIRS + pair
    zero = jnp.zeros_like(w1)
    route_ref[...] = jnp.concatenate(
        [bucket.astype(F32), jnp.where(i1 < i2, w1, w2), jnp.where(i1 < i2, w2, w1)] + [zero] * 5,
        axis=0)


def _first_index(rows, value):
    idx = jnp.full(value.shape, len(rows) - 1, I32)
    for j in range(len(rows) - 2, -1, -1):
        idx = jnp.where(rows[j] == value, j, idx)
    return idx


def _select_by_index(idx, rows):
    out = rows[-1]
    for j in range(len(rows) - 2, -1, -1):
        out = jnp.where(idx == j, rows[j], out)
    return out


def _post(x2d, o_gla, o_dsa, o_conv, w_out, nx, w_cq, qn_t, e_x, kbd, vbd, obd, w_co, nf, w_r, b_r,
          B, S, M, tm):
    T, D = x2d.shape
    nblk = S // tm
    const = lambda i: (0, 0)
    row = lambda i: (i, 0)
    bat = lambda i: (i // nblk, 0)
    HM = X_HEADS * M
    return pl.pallas_call(
        functools.partial(_post_kernel, mlen=M),
        out_shape=(jax.ShapeDtypeStruct((T, D), F32), jax.ShapeDtypeStruct((T, D), F32),
                   jax.ShapeDtypeStruct((8, T), F32)),
        grid=(T // tm,),
        in_specs=[
            pl.BlockSpec((tm, D), row),
            pl.BlockSpec((tm, GLA_VW), row),
            pl.BlockSpec((tm, DSA_HEADS * DSA_DH), row),
            pl.BlockSpec((tm, CONV_CH), row),
            pl.BlockSpec((D, D), const),
            pl.BlockSpec((1, D), const),
            pl.BlockSpec((D, XW), const),
            pl.BlockSpec((1, XW), const),
            pl.BlockSpec((XW, XW), const),
            pl.BlockSpec((HM, XW), bat),
            pl.BlockSpec((HM, XW), bat),
            pl.BlockSpec((HM, XW), const),
            pl.BlockSpec((XW, D), const),
            pl.BlockSpec((1, D), const),
            pl.BlockSpec((LANES, D), const),
            pl.BlockSpec((LANES, 1), const),
        ],
        out_specs=(pl.BlockSpec((tm, D), row), pl.BlockSpec((tm, D), row),
                   pl.BlockSpec((8, tm), lambda i: (0, i))),
        compiler_params=_cparams(("arbitrary",)),
        name="post",
    )(x2d, o_gla, o_dsa, o_conv, w_out, nx, w_cq, qn_t, e_x, kbd, vbd, obd, w_co, nf, w_r, b_r)


MOE_TR = 256
MOE_RANK_TL = 512
BUCKET_ROWS = 32


def _moe_rank_kernel(route_ref, triu_ref, rank_ref, count_ref, carry_ref):
    @pl.when(pl.program_id(0) == 0)
    def _():
        carry_ref[...] = jnp.zeros_like(carry_ref)

    bucket = route_ref[0:1, :].astype(I32)
    onehot = (lax.broadcasted_iota(I32, (BUCKET_ROWS, MOE_RANK_TL), 0) == bucket)
    incl = _dot(onehot.astype(BF16), triu_ref[...])
    carry = carry_ref[...]
    before = jnp.where(onehot, incl - 1.0 + carry[:, 0:1], 0.0)
    rank_ref[...] = jnp.sum(before, axis=0, keepdims=True).astype(I32)
    carry = carry + incl[:, MOE_RANK_TL - 1:MOE_RANK_TL]
    carry_ref[...] = carry
    count_ref[...] = carry


def _moe_rank(route):
    T = route.shape[1]
    tl = MOE_RANK_TL
    triu = jnp.asarray(np.triu(np.ones((tl, tl), np.float32)), BF16)
    return pl.pallas_call(
        _moe_rank_kernel,
        out_shape=(jax.ShapeDtypeStruct((1, T), I32), jax.ShapeDtypeStruct((BUCKET_ROWS, LANES), F32)),
        grid=(T // tl,),
        in_specs=[pl.BlockSpec((8, tl), lambda i: (0, i)), pl.BlockSpec((tl, tl), lambda i: (0, 0))],
        out_specs=(pl.BlockSpec((1, tl), lambda i: (0, i)),
                   pl.BlockSpec((BUCKET_ROWS, LANES), lambda i: (0, 0))),
        scratch_shapes=[pltpu.VMEM((BUCKET_ROWS, LANES), F32)],
        compiler_params=_cparams(("arbitrary",)),
        name="moe_rank",
    )(route, triu)


def _row_gather(idx_ref, first, src_ref, buf_ref, sem, rows, *, start, unroll=8):
    if not start:
        pltpu.make_async_copy(src_ref.at[pl.ds(0, rows)], buf_ref, sem).wait()
        return

    def body(r, carry):
        pltpu.make_async_copy(src_ref.at[pl.ds(idx_ref[first + r], 1)], buf_ref.at[pl.ds(r, 1)],
                              sem).start()
        return carry

    lax.fori_loop(0, rows, body, 0, unroll=unroll)


def _moe_experts_kernel(elo_ref, ehi_ref, used_ref, src_ref, hf_ref, g_ref, wgl_ref, wul_ref,
                        wdl_ref, wgh_ref, wuh_ref, wdh_ref, o_ref, buf_ref, sem, *, ntile):
    j = pl.program_id(0)
    tr = MOE_TR
    nxt = jnp.minimum(j + 1, ntile - 1)

    def gather(tile, start, unroll=8):
        slot = tile % 2 if start is False else (j + 1) % 2
        _row_gather(src_ref, tile * tr, hf_ref, buf_ref.at[slot], sem.at[slot], tr, start=start,
                    unroll=unroll)

    @pl.when(j == 0)
    def _():
        _row_gather(src_ref, 0, hf_ref, buf_ref.at[0], sem.at[0], tr, start=True)

    gather(j, False)
    used = used_ref[j] > 0

    @pl.when(jnp.logical_not(used))
    def _():
        gather(nxt, True)
        o_ref[...] = jnp.zeros_like(o_ref)

    @pl.when(used)
    def _():
        gather(nxt, True, unroll=True)
        t = buf_ref[j % 2].astype(BF16)
        g = g_ref[0].T
        y = None
        for col, (wg, wu, wd) in enumerate(((wgl_ref, wul_ref, wdl_ref), (wgh_ref, wuh_ref, wdh_ref))):
            a = _dot(t, wg[0])
            u = _dot(t, wu[0])
            he = (a / (1.0 + jnp.exp(-a))) * u * g[:, col:col + 1]
            part = _dot(he.astype(BF16), wd[0])
            y = part if y is None else y + part
        o_ref[...] = y

    @pl.when(j == ntile - 1)
    def _():
        _row_gather(src_ref, 0, hf_ref, buf_ref.at[(j + 1) % 2], sem.at[(j + 1) % 2], tr, start=False)


def _moe_experts(e_lo, e_hi, used, src, hf, gates, wg, wu, wd):
    T, D = hf.shape
    DE = wg.shape[-1]
    tr = MOE_TR
    ntile = src.shape[0] // tr
    lo = lambda j, el, eh, us, sr: (el[j], 0, 0)
    hi = lambda j, el, eh, us, sr: (eh[j], 0, 0)
    return pl.pallas_call(
        functools.partial(_moe_experts_kernel, ntile=ntile),
        out_shape=jax.ShapeDtypeStruct((ntile * tr, D), F32),
        grid_spec=pltpu.PrefetchScalarGridSpec(
            num_scalar_prefetch=4, grid=(ntile,),
            in_specs=[
                pl.BlockSpec(memory_space=pl.ANY),
                pl.BlockSpec((1, 8, tr), lambda j, el, eh, us, sr: (j, 0, 0)),
                pl.BlockSpec((1, D, DE), lo), pl.BlockSpec((1, D, DE), lo), pl.BlockSpec((1, DE, D), lo),
                pl.BlockSpec((1, D, DE), hi), pl.BlockSpec((1, D, DE), hi), pl.BlockSpec((1, DE, D), hi),
            ],
            out_specs=pl.BlockSpec((tr, D), lambda j, el, eh, us, sr: (j, 0)),
            scratch_shapes=[pltpu.VMEM((2, tr, D), F32), pltpu.SemaphoreType.DMA((2,))]),
        compiler_params=_cparams(("arbitrary",)),
        name="moe_experts",
    )(e_lo, e_hi, used, src, hf, gates, wg, wu, wd, wg, wu, wd)


def _moe_combine_kernel(slot_ref, x2_ref, ys_ref, o_ref, buf_ref, sem, *, tm, ntile):
    i = pl.program_id(0)

    def fetch(tile, slot, start):
        _row_gather(slot_ref, tile * tm, ys_ref, buf_ref.at[slot], sem.at[slot], tm, start=start)

    @pl.when(i == 0)
    def _():
        fetch(0, 0, True)

    @pl.when(i + 1 < ntile)
    def _():
        fetch(i + 1, (i + 1) % 2, True)

    fetch(i, i % 2, False)
    o_ref[...] = x2_ref[...] + buf_ref[i % 2]


def _moe_combine(slot, x2, ys, tm):
    T, D = x2.shape
    ntile = T // tm
    return pl.pallas_call(
        functools.partial(_moe_combine_kernel, tm=tm, ntile=ntile),
        out_shape=jax.ShapeDtypeStruct((T, D), F32),
        grid_spec=pltpu.PrefetchScalarGridSpec(
            num_scalar_prefetch=1, grid=(ntile,),
            in_specs=[pl.BlockSpec((tm, D), lambda i, sl: (i, 0)),
                      pl.BlockSpec(memory_space=pl.ANY)],
            out_specs=pl.BlockSpec((tm, D), lambda i, sl: (i, 0)),
            scratch_shapes=[pltpu.VMEM((2, tm, D), F32), pltpu.SemaphoreType.DMA((2,))]),
        compiler_params=_cparams(("arbitrary",)),
        name="moe_combine",
    )(slot, x2, ys)


def _moe(x2, hf, route, wg, wu, wd):
    T, D = x2.shape
    tr = MOE_TR
    ntile = T // tr + MOE_BUCKETS
    bucket = route[0].astype(I32)
    rank, counts = _moe_rank(route)
    counts = counts[:MOE_BUCKETS, 0].astype(I32)
    ntl = (counts + tr - 1) // tr
    tile_end = jnp.cumsum(ntl)
    start = (tile_end - ntl) * tr
    slot = start[bucket] + rank[0]
    src = jnp.zeros((ntile * tr,), I32).at[slot].set(jnp.arange(T, dtype=I32))
    tile_bucket = jnp.minimum(jnp.searchsorted(tile_end, jnp.arange(ntile, dtype=I32), side="right"),
                              MOE_BUCKETS - 1).astype(I32)
    used = (jnp.arange(ntile, dtype=I32) < tile_end[-1]).astype(I32)
    pair_lo = jnp.asarray([0, 0, 0, 1, 1, 2], I32)
    pair_hi = jnp.asarray([1, 2, 3, 2, 3, 3], I32)
    grp = tile_bucket // MOE_PAIRS
    e_lo = grp * EXPERTS_PER_GROUP + pair_lo[tile_bucket % MOE_PAIRS]
    e_hi = grp * EXPERTS_PER_GROUP + pair_hi[tile_bucket % MOE_PAIRS]
    gates = jnp.pad(route[1:3][:, src], ((0, 6), (0, 0)))
    gates = gates.reshape(8, ntile, tr).transpose(1, 0, 2)
    ys = _moe_experts(e_lo, e_hi, used, src, hf, gates, wg, wu, wd)
    return _moe_combine(slot, x2, ys, tm=tr)


def _pad_heads(w, heads, dh, slot):
    lead = w.shape[:-1]
    w = w.reshape(lead + (heads, dh))
    w = jnp.pad(w, [(0, 0)] * len(lead) + [(0, 0), (0, slot - dh)])
    return w.reshape(lead + (heads * slot,))


def _pad_cols(w, width):
    return jnp.pad(w, [(0, 0)] * (w.ndim - 1) + [(0, width - w.shape[-1])])


def _layout_w_in(w):
    sizes = (GLA_HEADS * GLA_DK, GLA_HEADS * GLA_DK, GLA_VW, GLA_RANK, GLA_VW,
             DSA_HEADS * DSA_DH, DSA_HEADS * DSA_DH, DSA_HEADS * DSA_DH, IQ_W, IDX_DIM, IDX_HEADS,
             CU_W)
    offs = np.cumsum((0,) + sizes)
    g_q, g_k, g_v, g_lr, g_r, d_q, d_k, d_v, i_q, i_k, i_w, c_u = [
        w[:, offs[n]:offs[n + 1]] for n in range(len(sizes))]
    cols = [
        _pad_heads(d_q, DSA_HEADS, DSA_DH, LANES),
        _pad_heads(d_k, DSA_HEADS, DSA_DH, LANES),
        _pad_heads(d_v, DSA_HEADS, DSA_DH, LANES),
        g_v, g_r,
        _pad_heads(g_q, GLA_HEADS, GLA_DK, GLA_DKP),
        _pad_heads(g_k, GLA_HEADS, GLA_DK, GLA_DKP),
        i_q, c_u,
        _pad_cols(g_lr, LANES),
        _pad_cols(jnp.concatenate([i_k, i_w], axis=-1), LANES),
    ]
    out = jnp.concatenate(cols, axis=-1)
    assert out.shape[-1] == NP_IN
    return out.astype(BF16)


def _block_ones(n, blk):
    idx = np.arange(n) // blk
    return jnp.asarray((idx[:, None] == idx[None, :]).astype(np.float32))


def _layer(x2d, mem2d, rel_bias, p, B, S, M):
    D = x2d.shape[1]
    row = lambda v: v.reshape(1, -1)
    ones_dv = np.zeros((1, DSA_W), np.float32)
    ones_dv[0, DSA_DH::LANES] = 1.0
    z = _in_proj(x2d, row(p["norm_mix"]), _layout_w_in(p["w_in"]),
                 _pad_cols(row(p["dsa_qnorm"]), LANES), _pad_cols(row(p["dsa_knorm"]), LANES),
                 jnp.asarray(ones_dv), tm=min(512, S))
    wa2_p = jnp.pad(_pad_heads(p["gla_wa2"], GLA_HEADS, GLA_DK, GLA_DKP),
                    ((0, LANES - GLA_RANK), (0, 0)))
    wa2_hi = wa2_p.astype(BF16)
    wa2_p = jnp.stack([wa2_hi, (wa2_p - wa2_hi.astype(F32)).astype(BF16)])
    o_gla = _gla(z, wa2_p, row(_pad_heads(p["gla_ba"], GLA_HEADS, GLA_DK, GLA_DKP)),
                 row(jnp.tile(p["gla_onorm"], GLA_HEADS)), B, S, rows=min(256, S))
    o_dsa = _dsa(z, rel_bias, B, S)
    o_conv = _conv(z, p["conv_w"], row(p["conv_b"]), row(p["conv_ln_g"]), row(p["conv_ln_b"]),
                   B, S, tm=min(512, S))
    e_x = _block_ones(XW, X_DH).astype(BF16)
    kbd, vbd = _mem_kv(mem2d, row(p["norm_mem"]), p["w_ckv"].astype(BF16),
                       row(jnp.tile(p["x_knorm"], X_HEADS)), e_x, B, M)
    obd = (jnp.arange(X_HEADS * M)[:, None] // M == jnp.arange(XW)[None, :] // X_DH).astype(BF16)
    w_r = _pad_cols(jnp.concatenate([p["w_re"], p["w_rg"]], axis=-1), LANES).T
    b_r = _pad_cols(row(jnp.concatenate([p["b_re"], p["b_rg"]])), LANES).T
    x2, hf, route = _post(
        x2d, o_gla, o_dsa, o_conv, p["w_out"].astype(BF16), row(p["norm_x"]),
        p["w_cq"].astype(BF16), row(jnp.tile(p["x_qnorm"], X_HEADS)), e_x, kbd, vbd, obd,
        p["w_co"].astype(BF16), row(p["norm_ffn"]), w_r, b_r, B, S, M, tm=min(512, S))
    return _moe(x2, hf, route, p["w_gate"].astype(BF16), p["w_up"].astype(BF16),
                p["w_down"].astype(BF16))


_LAYER_KEYS = ("norm_mix", "w_in", "gla_wa2", "gla_ba", "gla_onorm", "dsa_qnorm", "dsa_knorm",
               "conv_w", "conv_b", "conv_ln_g", "conv_ln_b", "w_out", "norm_x", "norm_mem", "w_cq",
               "w_ckv", "x_qnorm", "x_knorm", "w_co", "norm_ffn", "w_rg", "b_rg", "w_re", "b_re",
               "w_gate", "w_up", "w_down")


def kernel(x, mem, rel_bias, norm_mix, w_in, gla_wa2, gla_ba, gla_onorm, dsa_qnorm, dsa_knorm,
           conv_w, conv_b, conv_ln_g, conv_ln_b, w_out, norm_x, norm_mem, w_cq, w_ckv, x_qnorm,
           x_knorm, w_co, norm_ffn, w_rg, b_rg, w_re, b_re, w_gate, w_up, w_down):
    stacked = dict(zip(_LAYER_KEYS, (
        norm_mix, w_in, gla_wa2, gla_ba, gla_onorm, dsa_qnorm, dsa_knorm, conv_w, conv_b,
        conv_ln_g, conv_ln_b, w_out, norm_x, norm_mem, w_cq, w_ckv, x_qnorm, x_knorm, w_co,
        norm_ffn, w_rg, b_rg, w_re, b_re, w_gate, w_up, w_down)))
    B, S, D = x.shape
    M = mem.shape[1]
    x2d = x.reshape(B * S, D)
    mem2d = mem.reshape(B * M, D)
    for l in range(norm_mix.shape[0]):
        x2d = _layer(x2d, mem2d, rel_bias, {k: v[l] for k, v in stacked.items()}, B, S, M)
    return x2d.reshape(B, S, D)
```

```python
import functools
import math

import jax
import jax.numpy as jnp
import numpy as np
from jax import lax
from jax.experimental import pallas as pl
from jax.experimental.pallas import tpu as pltpu

F32 = jnp.float32
BF16 = jnp.bfloat16
I32 = jnp.int32

LANES = 128
NORM_EPS = 1e-6

GLA_HEADS = 4
GLA_DK = 48
GLA_DV = 96
GLA_DKP = 64
GLA_RANK = 16
GLA_TAU = 16.0
DSA_HEADS = 6
DSA_DH = 64
IDX_HEADS = 8
IDX_DIM = 64
TOPK_MAX = 256
CONV_CH = 256
CONV_WIDTH = 31
NUM_BUCKETS = 32
MAX_DISTANCE = 128
X_HEADS = 4
X_DH = 64
N_GROUPS = 4
EXPERTS_PER_GROUP = 4
N_EXPERTS = 16
MOE_PAIRS = 6
MOE_BUCKETS = N_GROUPS * MOE_PAIRS
MOE_ROUTE_ROWS = 3

DSA_W = DSA_HEADS * DSA_DH
DSA_WP = DSA_HEADS * LANES
GLA_VW = GLA_HEADS * GLA_DV
GLA_KW = GLA_HEADS * GLA_DKP
IQ_W = IDX_HEADS * IDX_DIM
CU_W = 2 * CONV_CH
OFF_DQ, OFF_DK, OFF_DV = 0, DSA_W, 2 * DSA_W
OFF_GV = 3 * DSA_W
OFF_GR = OFF_GV + GLA_VW
OFF_GQ = OFF_GR + GLA_VW + LANES
OFF_GK = OFF_GQ + GLA_KW
OFF_IQ = OFF_GK + GLA_KW
OFF_CU = OFF_IQ + IQ_W
OFF_LR = OFF_CU + CU_W
OFF_IK = OFF_LR + LANES
NP_IN = OFF_IK + LANES

NEG = -1e30
LOG2E = math.log2(math.e)
INT_MIN = -2 ** 31
INT_MAX = 2 ** 31 - 1

VMEM_LIMIT = 56 * 1024 * 1024


def _cparams(sem):
    return pltpu.CompilerParams(dimension_semantics=sem, vmem_limit_bytes=VMEM_LIMIT)


def _dot(a, b):
    return jnp.dot(a, b, preferred_element_type=F32)


def _dot_nt(a, b):
    return lax.dot_general(a, b, (((1,), (1,)), ((), ())), preferred_element_type=F32)


def _split_bf16(x, terms):
    parts = []
    for _ in range(terms):
        hi = x.astype(BF16)
        parts.append(hi)
        x = x - hi.astype(F32)
    return parts


def _dot_exact_lhs(a_bf16, b, terms):
    return functools.reduce(lambda u, v: u + v, [_dot(a_bf16, piece) for piece in _split_bf16(b, terms)])


def _dot_exact_rhs(a, b_bf16, terms):
    return functools.reduce(lambda u, v: u + v, [_dot(piece, b_bf16) for piece in _split_bf16(a, terms)])


def _rms(x, g):
    return x * lax.rsqrt(jnp.mean(x * x, axis=-1, keepdims=True) + NORM_EPS) * g


IN_CHUNK = 768
assert NP_IN % IN_CHUNK == 0
_IN_CHUNKS = tuple((c0, IN_CHUNK) for c0 in range(0, NP_IN, IN_CHUNK))


def _in_proj_kernel(x_ref, g_ref, w_ref, qk_gain_ref, o_ref):
    h = _rms(x_ref[...], g_ref[...]).astype(BF16)
    low_half = lax.broadcasted_iota(I32, (1, LANES), 1) < DSA_DH
    for c0, cw in _IN_CHUNKS:
        y = _dot(h, w_ref[:, c0:c0 + cw])
        if c0 == OFF_DQ:
            assert cw == 2 * DSA_W
            tiles = []
            for t0 in range(0, cw, LANES):
                yt = y[:, t0:t0 + LANES]
                sq = yt * yt
                s_lo = jnp.sum(jnp.where(low_half, sq, 0.0), axis=-1, keepdims=True)
                s_hi = jnp.sum(sq, axis=-1, keepdims=True) - s_lo
                inv = lax.rsqrt(jnp.where(low_half, s_lo, s_hi) * (1.0 / DSA_DH) + NORM_EPS)
                tiles.append(yt * inv)
            y = jnp.concatenate(tiles, axis=-1) * qk_gain_ref[...]
        o_ref[:, c0:c0 + cw] = y.astype(BF16)


def _in_proj(x2d, g, w_p, qk_gain, tm):
    T, D = x2d.shape
    return pl.pallas_call(
        _in_proj_kernel,
        out_shape=jax.ShapeDtypeStruct((T, NP_IN), BF16),
        grid=(T // tm,),
        in_specs=[
            pl.BlockSpec((tm, D), lambda i: (i, 0)),
            pl.BlockSpec((1, D), lambda i: (0, 0)),
            pl.BlockSpec((D, NP_IN), lambda i: (0, 0)),
            pl.BlockSpec((1, 2 * DSA_W), lambda i: (0, 0)),
        ],
        out_specs=pl.BlockSpec((tm, NP_IN), lambda i: (i, 0)),
        compiler_params=_cparams(("arbitrary",)),
        name="in_proj",
    )(x2d, g, w_p, qk_gain)


GLA_C = 64
GLA_SUB = 16
GLA_NSUB = GLA_C // GLA_SUB


def _gla_kernel(q_ref, k_ref, v_ref, r_ref, lr_ref, wa2_ref, ba_ref, on_ref, e_ref, e2_ref,
                et_ref, tril_ref, o_ref, st_ref, qf_ref, kf_ref, vf_ref, bf_ref, of_ref, *, rows):
    c = pl.program_id(1)

    @pl.when(c == 0)
    def _():
        st_ref[...] = jnp.zeros_like(st_ref)

    lr = lr_ref[...]
    pre = _dot(lr, wa2_ref[0]) + _dot(lr, wa2_ref[1]) + ba_ref[...]
    log_a = (jnp.minimum(pre, 0.0) - jnp.log(1.0 + jnp.exp(-jnp.abs(pre)))) * (LOG2E / GLA_TAU)
    bf_ref[...] = _dot_exact_lhs(tril_ref[...], log_a, 3)
    qf_ref[...] = q_ref[...].astype(F32) * (GLA_DK ** -0.5)
    kf_ref[...] = k_ref[...].astype(F32)
    vf_ref[...] = v_ref[...].astype(F32)
    e_mat = e_ref[...]
    row_id = lax.broadcasted_iota(I32, (GLA_SUB, GLA_KW), 0)

    def chunk(ci, carry):
        r0 = pl.multiple_of(ci * GLA_C, GLA_C)
        q = qf_ref[pl.ds(r0, GLA_C), :]
        k = kf_ref[pl.ds(r0, GLA_C), :]
        v = vf_ref[pl.ds(r0, GLA_C), :]
        b = bf_ref[pl.ds(r0, GLA_C), :]
        st = st_ref[...]
        o_inter = _dot_nt((q * jnp.exp2(b)).astype(BF16), st.astype(BF16))
        b_last = b[GLA_C - 1:GLA_C, :]
        kd = (k * jnp.exp2(b_last - b)).astype(BF16)
        upd = _dot(v.T.astype(BF16), kd) * et_ref[...]
        st_ref[...] = st * jnp.exp2(b_last) + upd
        outs = []
        for i in range(GLA_NSUB):
            qi = q[i * GLA_SUB:(i + 1) * GLA_SUB, :]
            bi = b[i * GLA_SUB:(i + 1) * GLA_SUB, :]
            acc = o_inter[i * GLA_SUB:(i + 1) * GLA_SUB, :]
            for j in range(i + 1):
                slabs = []
                for s in range(GLA_SUB):
                    row = r0 + (j * GLA_SUB + s)
                    ks = kf_ref[pl.ds(row, 1), :]
                    bs = bf_ref[pl.ds(row, 1), :]
                    if i == j:
                        p = jnp.where(row_id >= s, qi * ks * jnp.exp2(jnp.minimum(bi - bs, 0.0)), 0.0)
                    else:
                        p = qi * ks * jnp.exp2(bi - bs)
                    slabs.append(p.astype(BF16))
                pr = _dot(jnp.concatenate(slabs, axis=0), e_mat)
                for s in range(GLA_SUB):
                    vs = vf_ref[pl.ds(r0 + (j * GLA_SUB + s), 1), :]
                    acc = acc + pr[s * GLA_SUB:(s + 1) * GLA_SUB, :] * vs
            outs.append(acc)
        of_ref[pl.ds(r0, GLA_C), :] = jnp.concatenate(outs, axis=0)
        return carry

    lax.fori_loop(0, rows // GLA_C, chunk, 0)

    o = of_ref[...]
    ss = _dot_exact_rhs(o * o, e2_ref[...], 2) * (1.0 / GLA_DV)
    o = o * lax.rsqrt(ss + NORM_EPS) * on_ref[...]
    r = r_ref[...].astype(F32)
    o_ref[...] = (o * (r / (1.0 + jnp.exp(-r)))).astype(BF16)


def _gla(z, wa2_p, ba_p, on_p, B, S, rows):
    T = z.shape[0]
    nblk = S // rows
    e = np.zeros((GLA_KW, GLA_VW), np.float32)
    for h in range(GLA_HEADS):
        e[h * GLA_DKP:(h + 1) * GLA_DKP, h * GLA_DV:(h + 1) * GLA_DV] = 1.0
    e2 = np.zeros((GLA_VW, GLA_VW), np.float32)
    for h in range(GLA_HEADS):
        e2[h * GLA_DV:(h + 1) * GLA_DV, h * GLA_DV:(h + 1) * GLA_DV] = 1.0
    tr = np.zeros((rows, rows), np.float32)
    for cc in range(rows // GLA_C):
        tr[cc * GLA_C:(cc + 1) * GLA_C, cc * GLA_C:(cc + 1) * GLA_C] = np.tril(np.ones((GLA_C, GLA_C)))
    rowmap = lambda b, c: (b * nblk + c)
    const = lambda b, c: (0, 0)
    return pl.pallas_call(
        functools.partial(_gla_kernel, rows=rows),
        out_shape=jax.ShapeDtypeStruct((T, GLA_VW), BF16),
        grid=(B, nblk),
        in_specs=[
            pl.BlockSpec((rows, GLA_KW), lambda b, c: (rowmap(b, c), OFF_GQ // GLA_KW)),
            pl.BlockSpec((rows, GLA_KW), lambda b, c: (rowmap(b, c), OFF_GK // GLA_KW)),
            pl.BlockSpec((rows, GLA_VW), lambda b, c: (rowmap(b, c), OFF_GV // GLA_VW)),
            pl.BlockSpec((rows, GLA_VW), lambda b, c: (rowmap(b, c), OFF_GR // GLA_VW)),
            pl.BlockSpec((rows, LANES), lambda b, c: (rowmap(b, c), OFF_LR // LANES)),
            pl.BlockSpec((2, LANES, GLA_KW), lambda b, c: (0, 0, 0)),
            pl.BlockSpec((1, GLA_KW), const),
            pl.BlockSpec((1, GLA_VW), const),
            pl.BlockSpec((GLA_KW, GLA_VW), const),
            pl.BlockSpec((GLA_VW, GLA_VW), const),
            pl.BlockSpec((GLA_VW, GLA_KW), const),
            pl.BlockSpec((rows, rows), const),
        ],
        out_specs=pl.BlockSpec((rows, GLA_VW), lambda b, c: (rowmap(b, c), 0)),
        scratch_shapes=[
            pltpu.VMEM((GLA_VW, GLA_KW), F32),
            pltpu.VMEM((rows, GLA_KW), F32),
            pltpu.VMEM((rows, GLA_KW), F32),
            pltpu.VMEM((rows, GLA_VW), F32),
            pltpu.VMEM((rows, GLA_KW), F32),
            pltpu.VMEM((rows, GLA_VW), F32),
        ],
        compiler_params=_cparams(("arbitrary", "arbitrary")),
        name="gla",
    )(z, z, z, z, z, wa2_p, ba_p, on_p, jnp.asarray(e, BF16), jnp.asarray(e2, BF16), jnp.asarray(e.T),
      jnp.asarray(tr, BF16))


DSA_TQ = 256
DSA_VT_ROWS = LANES + 16


def _t5_bucket(dist):
    max_exact = NUM_BUCKETS // 2
    d = jnp.maximum(dist, 1).astype(F32)
    large = max_exact + (jnp.log(d / max_exact) / math.log(MAX_DISTANCE / max_exact)
                         * (NUM_BUCKETS - max_exact)).astype(I32)
    large = jnp.minimum(large, NUM_BUCKETS - 1)
    return jnp.where(dist < max_exact, dist, large)


def _block_loop(start, stop, body, init):
    npair = (stop - start) // 2
    carry = lax.fori_loop(0, npair, lambda i, c: body([start + 2 * i, start + 2 * i + 1], c), init)
    return lax.cond((stop - start) % 2 == 1, lambda c: body([stop - 1], c), lambda c: c, carry)


def _dsa_kernel(rb_ref, dq_ref, dk_ref, dv_ref, iq_ref, ik_ref, iw_ref, o_ref,
                bias_ref, key_ref, key16_ref, vt_ref, qp_ref, iqs_ref, lo_ref, hi_ref, clo_ref, chi_ref,
                m_ref, acc_ref, lga_ref, lgb_ref, *, seq, topk):
    TQ = DSA_TQ
    bi = pl.program_id(0)
    qb = pl.program_id(1)
    nkb = qb + 1
    log2_seq = int(math.log2(seq))

    s_row = lax.broadcasted_iota(I32, (TQ, TQ), 0)
    t_lane = lax.broadcasted_iota(I32, (TQ, TQ), 1)

    @pl.when((bi == 0) & (qb == 0))
    def _():
        for delta in range(2):
            bucket = _t5_bucket(jnp.maximum(delta * TQ + t_lane - s_row, 0))
            for hh in range(DSA_HEADS):
                tile = jnp.zeros((TQ, TQ), F32)
                for bk in range(NUM_BUCKETS):
                    tile = jnp.where(bucket == bk, rb_ref[bk, hh] * LOG2E, tile)
                bias_ref[hh, delta] = tile

    @pl.when(qb == 0)
    def _():
        ones_rows = (lax.broadcasted_iota(I32, (DSA_VT_ROWS - LANES, TQ), 0) == 0).astype(BF16)

        def prep(kb, carry):
            k0 = pl.multiple_of(kb * TQ, TQ)
            for pr in range(DSA_HEADS // 2):
                vt_ref[kb, pr, 0:LANES, :] = (
                    dv_ref[pl.ds(k0, TQ), pr * LANES:(pr + 1) * LANES].astype(F32).T.astype(BF16))
                vt_ref[kb, pr, LANES:DSA_VT_ROWS, :] = ones_rows
            return carry

        lax.fori_loop(0, seq // TQ, prep, 0)

    low_half = lax.broadcasted_iota(I32, (TQ, LANES), 1) < DSA_DH
    for hh in range(DSA_HEADS):
        pair = dq_ref[:, (hh // 2) * LANES:(hh // 2 + 1) * LANES]
        qp_ref[:, hh * LANES:(hh + 1) * LANES] = jnp.where(low_half == (hh % 2 == 0), pair,
                                                           jnp.zeros_like(pair))

    for hh in range(IDX_HEADS):
        iqs_ref[hh * TQ:(hh + 1) * TQ, :] = iq_ref[:, hh * IDX_DIM:(hh + 1) * IDX_DIM]
    w_t = iw_ref[...].astype(F32).T * ((IDX_HEADS ** -0.5) * (IDX_DIM ** -0.5))

    def score_blocks(kbs, carry):
        k0s = [pl.multiple_of(kb * TQ, TQ) for kb in kbs]
        dots = [_dot_nt(ik_ref[pl.ds(k0, TQ), :][:, :IDX_DIM], iqs_ref[...]) for k0 in k0s]
        for kb, k0, dt in zip(kbs, k0s, dots):
            sc = jnp.zeros((TQ, TQ), F32)
            for hh in range(IDX_HEADS):
                sc = sc + (jnp.maximum(dt[:, hh * TQ:(hh + 1) * TQ], 0.0)
                           * w_t[IDX_DIM + hh:IDX_DIM + hh + 1, :])
            bits = pltpu.bitcast(sc, I32)
            keys = bits ^ ((bits >> 31) & INT_MAX)
            causal = (k0 + s_row) <= (qb * TQ + t_lane)
            keys = jnp.where(causal, keys, INT_MIN)
            key_ref[kb] = keys
            key16_ref[kb] = (keys >> 16).astype(jnp.int16)
        return carry

    _block_loop(0, nkb, score_blocks, 0)

    ncausal = qb * TQ + lax.broadcasted_iota(I32, (1, TQ), 1) + 1
    short = ncausal <= topk

    def count_cols(pred, dtype, ref):
        rows = 8 * (4 // jnp.dtype(dtype).itemsize)

        def body(kbs, cnt):
            for kb in kbs:
                hit = jnp.where(pred(ref[kb], kb), jnp.ones((), dtype), jnp.zeros((), dtype))
                for r in range(TQ // rows):
                    cnt = cnt + hit[r * rows:(r + 1) * rows, :]
            return cnt

        cnt = _block_loop(0, nkb, body, jnp.zeros((rows, TQ), dtype))
        return jnp.sum(cnt.astype(I32), axis=0, keepdims=True)

    def bisect_step(state, count_ge):
        lo, hi, clo, chi = state
        mid = (lo >> 1) + (hi >> 1) + (lo & hi & 1)
        c = count_ge(mid)
        up = (c >= topk) & (mid != lo)
        down = c < topk
        return (jnp.where(up, mid, lo), jnp.where(down, mid, hi),
                jnp.where(up, c, clo), jnp.where(down, c, chi))

    def step16(_, state):
        return bisect_step(state, lambda mid: count_cols(
            lambda kk, kb: kk >= mid.astype(jnp.int16), jnp.int16, key16_ref))

    lo16, _, clo, chi = lax.fori_loop(
        0, 16, step16,
        (jnp.full((1, TQ), -2 ** 15 + 1, I32), jnp.full((1, TQ), 2 ** 15, I32),
         ncausal, jnp.zeros((1, TQ), I32)))
    lo16_h = lo16.astype(jnp.int16)

    def low_half(kbs, carry):
        for kb in kbs:
            low = ((key_ref[kb] & 0xFFFF) - 2 ** 15).astype(jnp.int16)
            key16_ref[kb] = jnp.where(key16_ref[kb] == lo16_h, low, jnp.int16(-2 ** 15))
        return carry

    _block_loop(0, nkb, low_half, 0)
    lo_ref[...] = jnp.full((1, TQ), -2 ** 15, I32)
    hi_ref[...] = jnp.full((1, TQ), 2 ** 15, I32)
    clo_ref[...] = clo
    chi_ref[...] = chi
    above = chi

    def active_count():
        open_q = ((clo_ref[...] != topk) & (hi_ref[...] - lo_ref[...] > 1)
                  & jnp.logical_not(short))
        return jnp.sum(open_q.astype(I32))

    def bis_body(state):
        it, _ = state
        st = (lo_ref[...], hi_ref[...], clo_ref[...], chi_ref[...])
        for _ in range(2):
            frozen = st[2] == topk
            new = bisect_step(st, lambda mid: above + count_cols(
                lambda kk, kb: kk >= mid.astype(jnp.int16), jnp.int16, key16_ref))
            st = tuple(jnp.where(frozen, o, n) for o, n in zip(st, new))
        lo_ref[...], hi_ref[...], clo_ref[...], chi_ref[...] = st
        return it + 1, active_count()

    lax.while_loop(lambda s_: (s_[0] < 9) & (s_[1] > 0), bis_body, (jnp.int32(0), active_count()))
    lo_ref[...] = jnp.where(short, INT_MIN + 1, (lo16 << 16) + (lo_ref[...] + 2 ** 15))

    tie_q = (clo_ref[...] > topk) & jnp.logical_not(short)

    @pl.when(jnp.sum(tie_q.astype(I32)) > 0)
    def _():
        thr = lo_ref[...]
        need = topk - chi_ref[...]
        plo = jnp.full((1, TQ), -1, I32)
        phi = jnp.full((1, TQ), seq - 1, I32)
        for _ in range(log2_seq):
            pm = (plo + phi) >> 1
            c = count_cols(lambda kk, kb: (kk == thr) & ((kb * TQ + s_row) <= pm), I32, key_ref)
            ok = c >= need
            phi = jnp.where(ok, pm, phi)
            plo = jnp.where(ok, plo, pm)
        cut = jnp.where(tie_q, phi, seq - 1)

        def demote(kb, carry):
            kk = key_ref[kb]
            drop = (kk == thr) & ((kb * TQ + s_row) > cut)
            key_ref[kb] = jnp.where(drop, INT_MIN, kk)
            return carry

        lax.fori_loop(0, nkb, demote, 0)

    m_ref[...] = jnp.full(m_ref.shape, NEG, F32)
    acc_ref[...] = jnp.zeros_like(acc_ref)
    thr_sel = lo_ref[...]

    heads = range(DSA_HEADS)

    def qk(kb, hh):
        k0 = pl.multiple_of(kb * TQ, TQ)
        return _dot_nt(dk_ref[pl.ds(k0, TQ), (hh // 2) * LANES:(hh // 2 + 1) * LANES],
                       qp_ref[:, hh * LANES:(hh + 1) * LANES])

    def softmax_update(kb, logits, shift):
        sel = key_ref[kb] >= thr_sel
        m_prev = [m_ref[hh] for hh in heads]
        acc_prev = [acc_ref[hh] for hh in heads]
        lg = [jnp.where(sel, logits[hh], NEG) for hh in heads]
        m_new = [jnp.maximum(m_prev[hh], jnp.max(lg[hh], axis=0, keepdims=True) + shift[hh])
                 for hh in heads]
        p = [jnp.exp2(lg[hh] - (m_new[hh] - shift[hh])).astype(BF16) for hh in heads]
        pv = [_dot(vt_ref[kb, hh // 2], p[hh]) for hh in heads]
        for hh in heads:
            acc_ref[hh] = acc_prev[hh] * jnp.exp2(m_prev[hh] - m_new[hh]) + pv[hh]
            m_ref[hh] = m_new[hh]

    n_far = jnp.maximum(qb - 1, 0)
    far_shift = [rb_ref[NUM_BUCKETS - 1, hh] * LOG2E for hh in heads]

    @pl.when(n_far > 0)
    def _():
        for hh in heads:
            lga_ref[hh] = qk(0, hh)

    def far_pair(i, carry):
        ka = 2 * i
        for hh in heads:
            lgb_ref[hh] = qk(ka + 1, hh)
        softmax_update(ka, [lga_ref[hh] for hh in heads], far_shift)
        for hh in heads:
            lga_ref[hh] = qk(jnp.minimum(ka + 2, n_far - 1), hh)
        softmax_update(ka + 1, [lgb_ref[hh] for hh in heads], far_shift)
        return carry

    lax.fori_loop(0, n_far // 2, far_pair, 0)

    @pl.when(n_far % 2 == 1)
    def _():
        softmax_update(n_far - 1, [lga_ref[hh] for hh in heads], far_shift)

    def near_body(kb, carry):
        softmax_update(kb, [qk(kb, hh) + bias_ref[hh, qb - kb] for hh in heads], [0.0] * DSA_HEADS)
        return carry

    lax.fori_loop(n_far, nkb, near_body, 0)

    outs = []
    for hh in range(DSA_HEADS):
        a = acc_ref[hh]
        r0 = (hh % 2) * DSA_DH
        outs.append(a[r0:r0 + DSA_DH, :] / a[LANES:LANES + 1, :])
    o_ref[...] = jnp.concatenate(outs, axis=0).T.astype(BF16)


def _dsa(z, rel_bias, B, S):
    T = z.shape[0]
    TQ = DSA_TQ
    nqb = S // TQ
    topk = min(TOPK_MAX, S // 4)
    qrow = lambda b, q: b * nqb + q
    return pl.pallas_call(
        functools.partial(_dsa_kernel, seq=S, topk=topk),
        out_shape=jax.ShapeDtypeStruct((T, DSA_HEADS * DSA_DH), BF16),
        grid=(B, nqb),
        in_specs=[
            pl.BlockSpec(memory_space=pltpu.SMEM),
            pl.BlockSpec((TQ, DSA_W), lambda b, q: (qrow(b, q), OFF_DQ // DSA_W)),
            pl.BlockSpec((S, DSA_W), lambda b, q: (b, OFF_DK // DSA_W)),
            pl.BlockSpec((S, DSA_W), lambda b, q: (b, OFF_DV // DSA_W)),
            pl.BlockSpec((TQ, IQ_W), lambda b, q: (qrow(b, q), OFF_IQ // IQ_W)),
            pl.BlockSpec((S, LANES), lambda b, q: (b, OFF_IK // LANES)),
            pl.BlockSpec((TQ, LANES), lambda b, q: (qrow(b, q), OFF_IK // LANES)),
        ],
        out_specs=pl.BlockSpec((TQ, DSA_HEADS * DSA_DH), lambda b, q: (qrow(b, q), 0)),
        scratch_shapes=[
            pltpu.VMEM((DSA_HEADS, 2, TQ, TQ), F32),
            pltpu.VMEM((S // TQ, TQ, TQ), I32),
            pltpu.VMEM((S // TQ, TQ, TQ), jnp.int16),
            pltpu.VMEM((S // TQ, DSA_HEADS // 2, DSA_VT_ROWS, TQ), BF16),
            pltpu.VMEM((TQ, DSA_WP), BF16),
            pltpu.VMEM((IDX_HEADS * TQ, IDX_DIM), BF16),
            pltpu.VMEM((1, TQ), I32),
            pltpu.VMEM((1, TQ), I32),
            pltpu.VMEM((1, TQ), I32),
            pltpu.VMEM((1, TQ), I32),
            pltpu.VMEM((DSA_HEADS, 1, TQ), F32),
            pltpu.VMEM((DSA_HEADS, DSA_VT_ROWS, TQ), F32),
            pltpu.VMEM((DSA_HEADS, TQ, TQ), F32),
            pltpu.VMEM((DSA_HEADS, TQ, TQ), F32),
        ],
        compiler_params=_cparams(("arbitrary", "arbitrary")),
        name="dsa",
    )(rel_bias, z, z, z, z, z, z)


CONV_HALO = 32


def _conv_kernel(u_ref, w_ref, cb_ref, g_ref, b_ref, o_ref, h_ref, *, tm):
    @pl.when(pl.program_id(1) == 0)
    def _():
        h_ref[0:CONV_HALO, :] = jnp.zeros((CONV_HALO, CONV_CH), F32)

    u = u_ref[...].astype(F32)
    a = u[:, :CONV_CH]
    g = u[:, CONV_CH:]
    h_ref[CONV_HALO:CONV_HALO + tm, :] = a / (1.0 + jnp.exp(-g))
    acc = jnp.zeros((tm, CONV_CH), F32) + cb_ref[...]
    base = CONV_HALO - (CONV_WIDTH - 1)
    for j in range(CONV_WIDTH):
        acc = acc + h_ref[base + j:base + j + tm, :] * w_ref[j:j + 1, :]
    h_ref[0:CONV_HALO, :] = h_ref[tm:tm + CONV_HALO, :]
    mu = jnp.mean(acc, axis=-1, keepdims=True)
    d = acc - mu
    var = jnp.mean(d * d, axis=-1, keepdims=True)
    y = d * lax.rsqrt(var + NORM_EPS) * g_ref[...] + b_ref[...]
    o_ref[...] = (y / (1.0 + jnp.exp(-y))).astype(BF16)


def _conv(z, w, cb, g, b, B, S, tm):
    T = z.shape[0]
    nblk = S // tm
    const = lambda bb, c: (0, 0)
    return pl.pallas_call(
        functools.partial(_conv_kernel, tm=tm),
        out_shape=jax.ShapeDtypeStruct((T, CONV_CH), BF16),
        grid=(B, nblk),
        in_specs=[
            pl.BlockSpec((tm, CU_W), lambda bb, c: (bb * nblk + c, OFF_CU // CU_W)),
            pl.BlockSpec((CONV_WIDTH, CONV_CH), const),
            pl.BlockSpec((1, CONV_CH), const),
            pl.BlockSpec((1, CONV_CH), const),
            pl.BlockSpec((1, CONV_CH), const),
        ],
        out_specs=pl.BlockSpec((tm, CONV_CH), lambda bb, c: (bb * nblk + c, 0)),
        scratch_shapes=[pltpu.VMEM((tm + CONV_HALO, CONV_CH), F32)],
        compiler_params=_cparams(("arbitrary", "arbitrary")),
        name="conv",
    )(z, w, cb, g, b)


XW = X_HEADS * X_DH


def _mem_kv_kernel(m_ref, g_ref, w_ref, kn_ref, e_ref, k_ref, v_ref, *, mlen):
    hm = _rms(m_ref[...], g_ref[...]).astype(BF16)
    kv = _dot(hm, w_ref[...])
    k = kv[:, :XW]
    v = kv[:, XW:]
    ss = _dot_exact_rhs(k * k, e_ref[...], 2) * (1.0 / X_DH)
    k = k * lax.rsqrt(ss + NORM_EPS) * kn_ref[...] * (X_DH ** -0.5)
    lane_head = lax.broadcasted_iota(I32, (mlen, XW), 1) // X_DH
    for hh in range(X_HEADS):
        msk = lane_head == hh
        k_ref[hh * mlen:(hh + 1) * mlen, :] = jnp.where(msk, k, 0.0).astype(BF16)
        v_ref[hh * mlen:(hh + 1) * mlen, :] = jnp.where(msk, v, 0.0).astype(BF16)


def _mem_kv(mem2d, g, w_ckv, kn_t, e_x, B, M):
    D = mem2d.shape[1]
    const = lambda b: (0, 0)
    out = jax.ShapeDtypeStruct((B * X_HEADS * M, XW), BF16)
    return pl.pallas_call(
        functools.partial(_mem_kv_kernel, mlen=M),
        out_shape=(out, out),
        grid=(B,),
        in_specs=[
            pl.BlockSpec((M, D), lambda b: (b, 0)),
            pl.BlockSpec((1, D), const),
            pl.BlockSpec((D, 2 * XW), const),
            pl.BlockSpec((1, XW), const),
            pl.BlockSpec((XW, XW), const),
        ],
        out_specs=(pl.BlockSpec((X_HEADS * M, XW), lambda b: (b, 0)),
                   pl.BlockSpec((X_HEADS * M, XW), lambda b: (b, 0))),
        compiler_params=_cparams(("arbitrary",)),
        name="mem_kv",
    )(mem2d, g, w_ckv, kn_t, e_x)


def _post_kernel(x_ref, og_ref, od_ref, oc_ref, wo_ref, nx_ref, wcq_ref, qn_ref, e_ref,
                 kbd_ref, vbd_ref, obd_ref, wco_ref, nf_ref, wr_ref, br_ref,
                 x2_ref, hf_ref, route_ref, *, mlen):
    wo = wo_ref
    x1 = (x_ref[...]
          + _dot(og_ref[...], wo[0:GLA_VW, :])
          + _dot(od_ref[...], wo[GLA_VW:2 * GLA_VW, :])
          + _dot(oc_ref[...], wo[2 * GLA_VW:, :]))
    hx = _rms(x1, nx_ref[...]).astype(BF16)
    q = _dot(hx, wcq_ref[...])
    ss = _dot_exact_rhs(q * q, e_ref[...], 2) * (1.0 / X_DH)
    q = (q * lax.rsqrt(ss + NORM_EPS) * qn_ref[...]).astype(BF16)
    logits = _dot_nt(q, kbd_ref[...])
    ps = []
    for hh in range(X_HEADS):
        lg = logits[:, hh * mlen:(hh + 1) * mlen]
        ps.append(jnp.exp(lg - jnp.max(lg, axis=-1, keepdims=True)))
    p = jnp.concatenate(ps, axis=-1).astype(BF16)
    o = _dot(p, vbd_ref[...]) / _dot(p, obd_ref[...])
    x2 = x1 + _dot(o.astype(BF16), wco_ref[...])
    x2_ref[...] = x2
    hf = _rms(x2, nf_ref[...])
    rl = lax.dot_general(wr_ref[...], hf, (((1,), (1,)), ((), ())), preferred_element_type=F32,
                         precision=lax.Precision.HIGHEST) + br_ref[...]
    row = lambda r: rl[r:r + 1, :]
    gl = [row(N_EXPERTS + g) for g in range(N_GROUPS)]
    g_max = functools.reduce(jnp.maximum, gl)
    g_sel = _first_index(gl, g_max)
    g_w = 1.0 / functools.reduce(lambda a, b: a + b, [jnp.exp(g - g_max) for g in gl])
    el = [_select_by_index(g_sel, [row(g * EXPERTS_PER_GROUP + j) for g in range(N_GROUPS)])
          for j in range(EXPERTS_PER_GROUP)]
    e1 = functools.reduce(jnp.maximum, el)
    i1 = _first_index(el, e1)
    rest = [jnp.where(i1 == j, NEG, el[j]) for j in range(EXPERTS_PER_GROUP)]
    e2 = functools.reduce(jnp.maximum, rest)
    i2 = _first_index(rest, e2)
    p2 = jnp.exp(e2 - e1)
    w1 = g_w / (1.0 + p2)
    w2 = g_w * p2 / (1.0 + p2)
    a = jnp.minimum(i1, i2)
    b = jnp.maximum(i1, i2)
    pair = jnp.where(a == 0, 0, jnp.where(a == 1, 3, 5)) + b - a - 1
    bucket = g_sel * MOE_PAIRS + pair
    rows = jnp.concatenate([bucket.astype(F32), jnp.where(i1 < i2, w1, w2), jnp.where(i1 < i2, w2, w1)],
                           axis=0)
    route_ref[...] = jnp.concatenate([rows, jnp.zeros((8 - MOE_ROUTE_ROWS, rows.shape[1]), F32)], axis=0)
    cols = jnp.concatenate([rows, jnp.zeros((LANES - MOE_ROUTE_ROWS, rows.shape[1]), F32)], axis=0).T
    hf_ref[...] = jnp.concatenate([hf, cols], axis=1)


def _first_index(rows, value):
    idx = jnp.full(value.shape, len(rows) - 1, I32)
    for j in range(len(rows) - 2, -1, -1):
        idx = jnp.where(rows[j] == value, j, idx)
    return idx


def _select_by_index(idx, rows):
    out = rows[-1]
    for j in range(len(rows) - 2, -1, -1):
        out = jnp.where(idx == j, rows[j], out)
    return out


def _post(x2d, o_gla, o_dsa, o_conv, w_out, nx, w_cq, qn_t, e_x, kbd, vbd, obd, w_co, nf, w_r, b_r,
          B, S, M, tm):
    T, D = x2d.shape
    nblk = S // tm
    const = lambda i: (0, 0)
    row = lambda i: (i, 0)
    bat = lambda i: (i // nblk, 0)
    HM = X_HEADS * M
    return pl.pallas_call(
        functools.partial(_post_kernel, mlen=M),
        out_shape=(jax.ShapeDtypeStruct((T, D), F32), jax.ShapeDtypeStruct((T, D + LANES), F32),
                   jax.ShapeDtypeStruct((8, T), F32)),
        grid=(T // tm,),
        in_specs=[
            pl.BlockSpec((tm, D), row),
            pl.BlockSpec((tm, GLA_VW), row),
            pl.BlockSpec((tm, DSA_HEADS * DSA_DH), row),
            pl.BlockSpec((tm, CONV_CH), row),
            pl.BlockSpec((D, D), const),
            pl.BlockSpec((1, D), const),
            pl.BlockSpec((D, XW), const),
            pl.BlockSpec((1, XW), const),
            pl.BlockSpec((XW, XW), const),
            pl.BlockSpec((HM, XW), bat),
            pl.BlockSpec((HM, XW), bat),
            pl.BlockSpec((HM, XW), const),
            pl.BlockSpec((XW, D), const),
            pl.BlockSpec((1, D), const),
            pl.BlockSpec((LANES, D), const),
            pl.BlockSpec((LANES, 1), const),
        ],
        out_specs=(pl.BlockSpec((tm, D), row), pl.BlockSpec((tm, D + LANES), row),
                   pl.BlockSpec((8, tm), lambda i: (0, i))),
        compiler_params=_cparams(("arbitrary",)),
        name="post",
    )(x2d, o_gla, o_dsa, o_conv, w_out, nx, w_cq, qn_t, e_x, kbd, vbd, obd, w_co, nf, w_r, b_r)


MOE_TR = 256
MOE_RANK_TL = 512
BUCKET_ROWS = 32


def _moe_rank_kernel(route_ref, triu_ref, rank_ref, count_ref, carry_ref):
    @pl.when(pl.program_id(0) == 0)
    def _():
        carry_ref[...] = jnp.zeros_like(carry_ref)

    bucket = route_ref[0:1, :].astype(I32)
    onehot = (lax.broadcasted_iota(I32, (BUCKET_ROWS, MOE_RANK_TL), 0) == bucket)
    incl = _dot(onehot.astype(BF16), triu_ref[...])
    carry = carry_ref[...]
    before = jnp.where(onehot, incl - 1.0 + carry[:, 0:1], 0.0)
    rank_ref[...] = jnp.sum(before, axis=0, keepdims=True).astype(I32)
    carry = carry + incl[:, MOE_RANK_TL - 1:MOE_RANK_TL]
    carry_ref[...] = carry
    count_ref[...] = carry


def _moe_rank(route):
    T = route.shape[1]
    tl = MOE_RANK_TL
    triu = jnp.asarray(np.triu(np.ones((tl, tl), np.float32)), BF16)
    return pl.pallas_call(
        _moe_rank_kernel,
        out_shape=(jax.ShapeDtypeStruct((1, T), I32), jax.ShapeDtypeStruct((BUCKET_ROWS, LANES), F32)),
        grid=(T // tl,),
        in_specs=[pl.BlockSpec((8, tl), lambda i: (0, i)), pl.BlockSpec((tl, tl), lambda i: (0, 0))],
        out_specs=(pl.BlockSpec((1, tl), lambda i: (0, i)),
                   pl.BlockSpec((BUCKET_ROWS, LANES), lambda i: (0, 0))),
        scratch_shapes=[pltpu.VMEM((BUCKET_ROWS, LANES), F32)],
        compiler_params=_cparams(("arbitrary",)),
        name="moe_rank",
    )(route, triu)


def _moe_scatter_kernel(slot_ref, h_ref, xs0_ref, xs_ref, sem, *, tm):
    del xs0_ref
    base = pl.program_id(0) * tm

    def body(r, carry):
        pltpu.make_async_copy(h_ref.at[pl.ds(r, 1)], xs_ref.at[pl.ds(slot_ref[base + r], 1)], sem).start()
        return carry

    lax.fori_loop(0, tm, body, 0, unroll=8)
    pltpu.make_async_copy(h_ref, xs_ref.at[pl.ds(0, tm)], sem).wait()


def _moe_scatter(slot, h, nrows, tm):
    T, W = h.shape
    return pl.pallas_call(
        functools.partial(_moe_scatter_kernel, tm=tm),
        out_shape=jax.ShapeDtypeStruct((nrows, W), h.dtype),
        grid_spec=pltpu.PrefetchScalarGridSpec(
            num_scalar_prefetch=1, grid=(T // tm,),
            in_specs=[pl.BlockSpec((tm, W), lambda i, sl: (i, 0)),
                      pl.BlockSpec(memory_space=pl.ANY)],
            out_specs=pl.BlockSpec(memory_space=pl.ANY),
            scratch_shapes=[pltpu.SemaphoreType.DMA(())]),
        input_output_aliases={2: 0},
        compiler_params=_cparams(("arbitrary",)),
        name="moe_scatter",
    )(slot, h, jnp.zeros((nrows, W), h.dtype))


def _moe_experts_kernel(elo_ref, ehi_ref, used_ref, xs_ref, wgl_ref, wul_ref, wdl_ref,
                        wgh_ref, wuh_ref, wdh_ref, o_ref, *, d_model):
    used = used_ref[pl.program_id(0)] > 0

    @pl.when(jnp.logical_not(used))
    def _():
        o_ref[...] = jnp.zeros_like(o_ref)

    @pl.when(used)
    def _():
        t = xs_ref[:, :d_model].astype(BF16)
        y = None
        for col, (wg, wu, wd) in enumerate(((wgl_ref, wul_ref, wdl_ref), (wgh_ref, wuh_ref, wdh_ref))):
            g = xs_ref[:, d_model + 1 + col:d_model + 2 + col]
            a = _dot(t, wg[0])
            u = _dot(t, wu[0])
            he = (a / (1.0 + jnp.exp(-a))) * u * g
            part = _dot(he.astype(BF16), wd[0])
            y = part if y is None else y + part
        o_ref[...] = y


def _moe_experts(e_lo, e_hi, used, xs, wg, wu, wd):
    nrows, W = xs.shape
    D, DE = wg.shape[-2:]
    tr = MOE_TR
    lo = lambda j, el, eh, us: (el[j], 0, 0)
    hi = lambda j, el, eh, us: (eh[j], 0, 0)
    return pl.pallas_call(
        functools.partial(_moe_experts_kernel, d_model=D),
        out_shape=jax.ShapeDtypeStruct((nrows, D), F32),
        grid_spec=pltpu.PrefetchScalarGridSpec(
            num_scalar_prefetch=3, grid=(nrows // tr,),
            in_specs=[
                pl.BlockSpec((tr, W), lambda j, el, eh, us: (j, 0)),
                pl.BlockSpec((1, D, DE), lo), pl.BlockSpec((1, D, DE), lo), pl.BlockSpec((1, DE, D), lo),
                pl.BlockSpec((1, D, DE), hi), pl.BlockSpec((1, D, DE), hi), pl.BlockSpec((1, DE, D), hi),
            ],
            out_specs=pl.BlockSpec((tr, D), lambda j, el, eh, us: (j, 0))),
        compiler_params=_cparams(("arbitrary",)),
        name="moe_experts",
    )(e_lo, e_hi, used, xs, wg, wu, wd, wg, wu, wd)


def _row_gather(idx_ref, first, src_ref, buf_ref, sem, rows, *, start):
    if not start:
        pltpu.make_async_copy(src_ref.at[pl.ds(0, rows)], buf_ref, sem).wait()
        return

    def body(r, carry):
        pltpu.make_async_copy(src_ref.at[pl.ds(idx_ref[first + r], 1)], buf_ref.at[pl.ds(r, 1)],
                              sem).start()
        return carry

    lax.fori_loop(0, rows, body, 0, unroll=8)


def _moe_combine_kernel(slot_ref, x2_ref, ys_ref, o_ref, buf_ref, sem, *, tm, ntile):
    i = pl.program_id(0)

    def fetch(tile, slot, start):
        _row_gather(slot_ref, tile * tm, ys_ref, buf_ref.at[slot], sem.at[slot], tm, start=start)

    @pl.when(i == 0)
    def _():
        fetch(0, 0, True)

    @pl.when(i + 1 < ntile)
    def _():
        fetch(i + 1, (i + 1) % 2, True)

    fetch(i, i % 2, False)
    o_ref[...] = x2_ref[...] + buf_ref[i % 2]


def _moe_combine(slot, x2, ys, tm):
    T, D = x2.shape
    ntile = T // tm
    return pl.pallas_call(
        functools.partial(_moe_combine_kernel, tm=tm, ntile=ntile),
        out_shape=jax.ShapeDtypeStruct((T, D), F32),
        grid_spec=pltpu.PrefetchScalarGridSpec(
            num_scalar_prefetch=1, grid=(ntile,),
            in_specs=[pl.BlockSpec((tm, D), lambda i, sl: (i, 0)),
                      pl.BlockSpec(memory_space=pl.ANY)],
            out_specs=pl.BlockSpec((tm, D), lambda i, sl: (i, 0)),
            scratch_shapes=[pltpu.VMEM((2, tm, D), F32), pltpu.SemaphoreType.DMA((2,))]),
        compiler_params=_cparams(("arbitrary",)),
        name="moe_combine",
    )(slot, x2, ys)


def _moe(x2, hf_ext, route, wg, wu, wd):
    T, D = x2.shape
    tr = MOE_TR
    ntile = T // tr + MOE_BUCKETS
    bucket = route[0].astype(I32)
    rank, counts = _moe_rank(route)
    counts = counts[:MOE_BUCKETS, 0].astype(I32)
    ntl = (counts + tr - 1) // tr
    tile_end = jnp.cumsum(ntl)
    start = (tile_end - ntl) * tr
    onehot = bucket[:, None] == jnp.arange(MOE_BUCKETS, dtype=I32)[None, :]
    slot = jnp.sum(jnp.where(onehot, start[None, :], 0), axis=1) + rank[0]
    tile_id = jnp.arange(ntile, dtype=I32)
    tile_bucket = jnp.minimum(jnp.sum(tile_id[:, None] >= tile_end[None, :], axis=1),
                              MOE_BUCKETS - 1).astype(I32)
    used = (tile_id < tile_end[-1]).astype(I32)
    pair = tile_bucket % MOE_PAIRS
    pair_lo = (pair >= 3).astype(I32) + (pair >= 5).astype(I32)
    pair_hi = pair - jnp.where(pair_lo == 0, 0, jnp.where(pair_lo == 1, 3, 5)) + pair_lo + 1
    e_lo = (tile_bucket // MOE_PAIRS) * EXPERTS_PER_GROUP + pair_lo
    e_hi = (tile_bucket // MOE_PAIRS) * EXPERTS_PER_GROUP + pair_hi
    xs = _moe_scatter(slot, hf_ext, ntile * tr, tm=tr)
    ys = _moe_experts(e_lo, e_hi, used, xs, wg, wu, wd)
    return _moe_combine(slot, x2, ys, tm=tr)


def _pad_heads(w, heads, dh, slot):
    lead = w.shape[:-1]
    w = w.reshape(lead + (heads, dh))
    w = jnp.pad(w, [(0, 0)] * len(lead) + [(0, 0), (0, slot - dh)])
    return w.reshape(lead + (heads * slot,))


def _pad_cols(w, width):
    return jnp.pad(w, [(0, 0)] * (w.ndim - 1) + [(0, width - w.shape[-1])])


def _layout_w_in(w):
    sizes = (GLA_HEADS * GLA_DK, GLA_HEADS * GLA_DK, GLA_VW, GLA_RANK, GLA_VW,
             DSA_HEADS * DSA_DH, DSA_HEADS * DSA_DH, DSA_HEADS * DSA_DH, IQ_W, IDX_DIM, IDX_HEADS,
             CU_W)
    offs = np.cumsum((0,) + sizes)
    g_q, g_k, g_v, g_lr, g_r, d_q, d_k, d_v, i_q, i_k, i_w, c_u = [
        w[:, offs[n]:offs[n + 1]] for n in range(len(sizes))]
    cols = [
        d_q, d_k, d_v, g_v, _pad_cols(g_r, GLA_VW + LANES),
        _pad_heads(g_q, GLA_HEADS, GLA_DK, GLA_DKP),
        _pad_heads(g_k, GLA_HEADS, GLA_DK, GLA_DKP),
        i_q, c_u,
        _pad_cols(g_lr, LANES),
        _pad_cols(jnp.concatenate([i_k, i_w], axis=-1), LANES),
    ]
    out = jnp.concatenate(cols, axis=-1)
    assert out.shape[-1] == NP_IN
    return out.astype(BF16)


def _block_ones(n, blk):
    idx = np.arange(n) // blk
    return jnp.asarray((idx[:, None] == idx[None, :]).astype(np.float32))


def _layer(x2d, mem2d, rel_bias, p, B, S, M):
    D = x2d.shape[1]
    row = lambda v: v.reshape(1, -1)
    qk_gain = jnp.concatenate([jnp.tile(p["dsa_qnorm"], DSA_HEADS) * (DSA_DH ** -0.5 * LOG2E),
                               jnp.tile(p["dsa_knorm"], DSA_HEADS)])
    z = _in_proj(x2d, row(p["norm_mix"]), _layout_w_in(p["w_in"]), row(qk_gain), tm=min(512, S))
    wa2_p = jnp.pad(_pad_heads(p["gla_wa2"], GLA_HEADS, GLA_DK, GLA_DKP),
                    ((0, LANES - GLA_RANK), (0, 0)))
    wa2_hi = wa2_p.astype(BF16)
    wa2_p = jnp.stack([wa2_hi, (wa2_p - wa2_hi.astype(F32)).astype(BF16)])
    o_gla = _gla(z, wa2_p, row(_pad_heads(p["gla_ba"], GLA_HEADS, GLA_DK, GLA_DKP)),
                 row(jnp.tile(p["gla_onorm"], GLA_HEADS)), B, S, rows=min(256, S))
    o_dsa = _dsa(z, rel_bias, B, S)
    o_conv = _conv(z, p["conv_w"], row(p["conv_b"]), row(p["conv_ln_g"]), row(p["conv_ln_b"]),
                   B, S, tm=min(512, S))
    e_x = _block_ones(XW, X_DH).astype(BF16)
    kbd, vbd = _mem_kv(mem2d, row(p["norm_mem"]), p["w_ckv"].astype(BF16),
                       row(jnp.tile(p["x_knorm"], X_HEADS)), e_x, B, M)
    obd = (jnp.arange(X_HEADS * M)[:, None] // M == jnp.arange(XW)[None, :] // X_DH).astype(BF16)
    w_r = _pad_cols(jnp.concatenate([p["w_re"], p["w_rg"]], axis=-1), LANES).T
    b_r = _pad_cols(row(jnp.concatenate([p["b_re"], p["b_rg"]])), LANES).T
    x2, hf, route = _post(
        x2d, o_gla, o_dsa, o_conv, p["w_out"].astype(BF16), row(p["norm_x"]),
        p["w_cq"].astype(BF16), row(jnp.tile(p["x_qnorm"], X_HEADS)), e_x, kbd, vbd, obd,
        p["w_co"].astype(BF16), row(p["norm_ffn"]), w_r, b_r, B, S, M, tm=min(512, S))
    return _moe(x2, hf, route, p["w_gate"].astype(BF16), p["w_up"].astype(BF16),
                p["w_down"].astype(BF16))


_LAYER_KEYS = ("norm_mix", "w_in", "gla_wa2", "gla_ba", "gla_onorm", "dsa_qnorm", "dsa_knorm",
               "conv_w", "conv_b", "conv_ln_g", "conv_ln_b", "w_out", "norm_x", "norm_mem", "w_cq",
               "w_ckv", "x_qnorm", "x_knorm", "w_co", "norm_ffn", "w_rg", "b_rg", "w_re", "b_re",
               "w_gate", "w_up", "w_down")


def kernel(x, mem, rel_bias, norm_mix, w_in, gla_wa2, gla_ba, gla_onorm, dsa_qnorm, dsa_knorm,
           conv_w, conv_b, conv_ln_g, conv_ln_b, w_out, norm_x, norm_mem, w_cq, w_ckv, x_qnorm,
           x_knorm, w_co, norm_ffn, w_rg, b_rg, w_re, b_re, w_gate, w_up, w_down):
    stacked = dict(zip(_LAYER_KEYS, (
        norm_mix, w_in, gla_wa2, gla_ba, gla_onorm, dsa_qnorm, dsa_knorm, conv_w, conv_b,
        conv_ln_g, conv_ln_b, w_out, norm_x, norm_mem, w_cq, w_ckv, x_qnorm, x_knorm, w_co,
        norm_ffn, w_rg, b_rg, w_re, b_re, w_gate, w_up, w_down)))
    B, S, D = x.shape
    M = mem.shape[1]
    x2d = x.reshape(B * S, D)
    mem2d = mem.reshape(B * M, D)
    for l in range(norm_mix.shape[0]):
        x2d = _layer(x2d, mem2d, rel_bias, {k: v[l] for k, v in stacked.items()}, B, S, M)
    return x2d.reshape(B, S, D)
```

```python
import functools
import math

import jax
import jax.numpy as jnp
import numpy as np
from jax import lax
from jax.experimental import pallas as pl
from jax.experimental.pallas import tpu as pltpu

F32 = jnp.float32
BF16 = jnp.bfloat16
I32 = jnp.int32

LANES = 128
NORM_EPS = 1e-6

GLA_HEADS = 4
GLA_DK = 48
GLA_DV = 96
GLA_DKP = 64
GLA_RANK = 16
GLA_TAU = 16.0
DSA_HEADS = 6
DSA_DH = 64
IDX_HEADS = 8
IDX_DIM = 64
TOPK_MAX = 256
CONV_CH = 256
CONV_WIDTH = 31
NUM_BUCKETS = 32
MAX_DISTANCE = 128
X_HEADS = 4
X_DH = 64
N_GROUPS = 4
EXPERTS_PER_GROUP = 4
N_EXPERTS = 16
MOE_PAIRS = 6
MOE_BUCKETS = N_GROUPS * MOE_PAIRS
MOE_ROUTE_ROWS = 3

DSA_W = DSA_HEADS * DSA_DH
DSA_WP = DSA_HEADS * LANES
GLA_VW = GLA_HEADS * GLA_DV
GLA_KW = GLA_HEADS * GLA_DKP
IQ_W = IDX_HEADS * IDX_DIM
CU_W = 2 * CONV_CH
OFF_DQ, OFF_DK, OFF_DV = 0, DSA_W, 2 * DSA_W
OFF_GV = 3 * DSA_W
OFF_GR = OFF_GV + GLA_VW
OFF_GQ = OFF_GR + GLA_VW + LANES
OFF_GK = OFF_GQ + GLA_KW
OFF_IQ = OFF_GK + GLA_KW
OFF_CU = OFF_IQ + IQ_W
OFF_LR = OFF_CU + CU_W
OFF_IK = OFF_LR + LANES
NP_IN = OFF_IK + LANES

NEG = -1e30
LOG2E = math.log2(math.e)
INT_MIN = -2 ** 31
INT_MAX = 2 ** 31 - 1

VMEM_LIMIT = 56 * 1024 * 1024


def _cparams(sem):
    return pltpu.CompilerParams(dimension_semantics=sem, vmem_limit_bytes=VMEM_LIMIT)


def _dot(a, b):
    return jnp.dot(a, b, preferred_element_type=F32)


def _dot_nt(a, b):
    return lax.dot_general(a, b, (((1,), (1,)), ((), ())), preferred_element_type=F32)


def _split_bf16(x, terms):
    parts = []
    for _ in range(terms):
        hi = x.astype(BF16)
        parts.append(hi)
        x = x - hi.astype(F32)
    return parts


def _dot_exact_lhs(a_bf16, b, terms):
    return functools.reduce(lambda u, v: u + v, [_dot(a_bf16, piece) for piece in _split_bf16(b, terms)])


def _dot_exact_rhs(a, b_bf16, terms):
    return functools.reduce(lambda u, v: u + v, [_dot(piece, b_bf16) for piece in _split_bf16(a, terms)])


def _rms(x, g):
    return x * lax.rsqrt(jnp.mean(x * x, axis=-1, keepdims=True) + NORM_EPS) * g


IN_CHUNK = 768
assert NP_IN % IN_CHUNK == 0
_IN_CHUNKS = tuple((c0, IN_CHUNK) for c0 in range(0, NP_IN, IN_CHUNK))


def _in_proj_kernel(x_ref, g_ref, w_ref, qk_gain_ref, o_ref):
    h = _rms(x_ref[...], g_ref[...]).astype(BF16)
    low_half = lax.broadcasted_iota(I32, (1, LANES), 1) < DSA_DH
    for c0, cw in _IN_CHUNKS:
        y = _dot(h, w_ref[:, c0:c0 + cw])
        if c0 == OFF_DQ:
            assert cw == 2 * DSA_W
            tiles = []
            for t0 in range(0, cw, LANES):
                yt = y[:, t0:t0 + LANES]
                sq = yt * yt
                s_lo = jnp.sum(jnp.where(low_half, sq, 0.0), axis=-1, keepdims=True)
                s_hi = jnp.sum(sq, axis=-1, keepdims=True) - s_lo
                inv = lax.rsqrt(jnp.where(low_half, s_lo, s_hi) * (1.0 / DSA_DH) + NORM_EPS)
                tiles.append(yt * inv)
            y = jnp.concatenate(tiles, axis=-1) * qk_gain_ref[...]
        o_ref[:, c0:c0 + cw] = y.astype(BF16)


def _in_proj(x2d, g, w_p, qk_gain, tm):
    T, D = x2d.shape
    return pl.pallas_call(
        _in_proj_kernel,
        out_shape=jax.ShapeDtypeStruct((T, NP_IN), BF16),
        grid=(T // tm,),
        in_specs=[
            pl.BlockSpec((tm, D), lambda i: (i, 0)),
            pl.BlockSpec((1, D), lambda i: (0, 0)),
            pl.BlockSpec((D, NP_IN), lambda i: (0, 0)),
            pl.BlockSpec((1, 2 * DSA_W), lambda i: (0, 0)),
        ],
        out_specs=pl.BlockSpec((tm, NP_IN), lambda i: (i, 0)),
        compiler_params=_cparams(("arbitrary",)),
        name="in_proj",
    )(x2d, g, w_p, qk_gain)


GLA_C = 64
GLA_SUB = 16
GLA_NSUB = GLA_C // GLA_SUB


def _gla_kernel(q_ref, k_ref, v_ref, r_ref, lr_ref, wa2_ref, ba_ref, on_ref, e_ref, e2_ref,
                et_ref, tril_ref, o_ref, st_ref, qf_ref, kf_ref, vf_ref, bf_ref, of_ref, *, rows):
    c = pl.program_id(1)

    @pl.when(c == 0)
    def _():
        st_ref[...] = jnp.zeros_like(st_ref)

    lr = lr_ref[...]
    pre = _dot(lr, wa2_ref[0]) + _dot(lr, wa2_ref[1]) + ba_ref[...]
    log_a = (jnp.minimum(pre, 0.0) - jnp.log(1.0 + jnp.exp(-jnp.abs(pre)))) * (LOG2E / GLA_TAU)
    bf_ref[...] = _dot_exact_lhs(tril_ref[...], log_a, 3)
    qf_ref[...] = q_ref[...].astype(F32) * (GLA_DK ** -0.5)
    kf_ref[...] = k_ref[...].astype(F32)
    vf_ref[...] = v_ref[...].astype(F32)
    e_mat = e_ref[...]
    row_id = lax.broadcasted_iota(I32, (GLA_SUB, GLA_KW), 0)

    def chunk(ci, carry):
        r0 = pl.multiple_of(ci * GLA_C, GLA_C)
        q = qf_ref[pl.ds(r0, GLA_C), :]
        k = kf_ref[pl.ds(r0, GLA_C), :]
        v = vf_ref[pl.ds(r0, GLA_C), :]
        b = bf_ref[pl.ds(r0, GLA_C), :]
        st = st_ref[...]
        o_inter = _dot_nt((q * jnp.exp2(b)).astype(BF16), st.astype(BF16))
        b_last = b[GLA_C - 1:GLA_C, :]
        kd = (k * jnp.exp2(b_last - b)).astype(BF16)
        upd = _dot(v.T.astype(BF16), kd) * et_ref[...]
        st_ref[...] = st * jnp.exp2(b_last) + upd
        outs = []
        for i in range(GLA_NSUB):
            qi = q[i * GLA_SUB:(i + 1) * GLA_SUB, :]
            bi = b[i * GLA_SUB:(i + 1) * GLA_SUB, :]
            acc = o_inter[i * GLA_SUB:(i + 1) * GLA_SUB, :]
            for j in range(i + 1):
                slabs = []
                for s in range(GLA_SUB):
                    row = r0 + (j * GLA_SUB + s)
                    ks = kf_ref[pl.ds(row, 1), :]
                    bs = bf_ref[pl.ds(row, 1), :]
                    if i == j:
                        p = jnp.where(row_id >= s, qi * ks * jnp.exp2(jnp.minimum(bi - bs, 0.0)), 0.0)
                    else:
                        p = qi * ks * jnp.exp2(bi - bs)
                    slabs.append(p.astype(BF16))
                pr = _dot(jnp.concatenate(slabs, axis=0), e_mat)
                for s in range(GLA_SUB):
                    vs = vf_ref[pl.ds(r0 + (j * GLA_SUB + s), 1), :]
                    acc = acc + pr[s * GLA_SUB:(s + 1) * GLA_SUB, :] * vs
            outs.append(acc)
        of_ref[pl.ds(r0, GLA_C), :] = jnp.concatenate(outs, axis=0)
        return carry

    lax.fori_loop(0, rows // GLA_C, chunk, 0)

    o = of_ref[...]
    ss = _dot_exact_rhs(o * o, e2_ref[...], 2) * (1.0 / GLA_DV)
    o = o * lax.rsqrt(ss + NORM_EPS) * on_ref[...]
    r = r_ref[...].astype(F32)
    o_ref[...] = (o * (r / (1.0 + jnp.exp(-r)))).astype(BF16)


def _gla(z, wa2_p, ba_p, on_p, B, S, rows):
    T = z.shape[0]
    nblk = S // rows
    e = np.zeros((GLA_KW, GLA_VW), np.float32)
    for h in range(GLA_HEADS):
        e[h * GLA_DKP:(h + 1) * GLA_DKP, h * GLA_DV:(h + 1) * GLA_DV] = 1.0
    e2 = np.zeros((GLA_VW, GLA_VW), np.float32)
    for h in range(GLA_HEADS):
        e2[h * GLA_DV:(h + 1) * GLA_DV, h * GLA_DV:(h + 1) * GLA_DV] = 1.0
    tr = np.zeros((rows, rows), np.float32)
    for cc in range(rows // GLA_C):
        tr[cc * GLA_C:(cc + 1) * GLA_C, cc * GLA_C:(cc + 1) * GLA_C] = np.tril(np.ones((GLA_C, GLA_C)))
    rowmap = lambda b, c: (b * nblk + c)
    const = lambda b, c: (0, 0)
    return pl.pallas_call(
        functools.partial(_gla_kernel, rows=rows),
        out_shape=jax.ShapeDtypeStruct((T, GLA_VW), BF16),
        grid=(B, nblk),
        in_specs=[
            pl.BlockSpec((rows, GLA_KW), lambda b, c: (rowmap(b, c), OFF_GQ // GLA_KW)),
            pl.BlockSpec((rows, GLA_KW), lambda b, c: (rowmap(b, c), OFF_GK // GLA_KW)),
            pl.BlockSpec((rows, GLA_VW), lambda b, c: (rowmap(b, c), OFF_GV // GLA_VW)),
            pl.BlockSpec((rows, GLA_VW), lambda b, c: (rowmap(b, c), OFF_GR // GLA_VW)),
            pl.BlockSpec((rows, LANES), lambda b, c: (rowmap(b, c), OFF_LR // LANES)),
            pl.BlockSpec((2, LANES, GLA_KW), lambda b, c: (0, 0, 0)),
            pl.BlockSpec((1, GLA_KW), const),
            pl.BlockSpec((1, GLA_VW), const),
            pl.BlockSpec((GLA_KW, GLA_VW), const),
            pl.BlockSpec((GLA_VW, GLA_VW), const),
            pl.BlockSpec((GLA_VW, GLA_KW), const),
            pl.BlockSpec((rows, rows), const),
        ],
        out_specs=pl.BlockSpec((rows, GLA_VW), lambda b, c: (rowmap(b, c), 0)),
        scratch_shapes=[
            pltpu.VMEM((GLA_VW, GLA_KW), F32),
            pltpu.VMEM((rows, GLA_KW), F32),
            pltpu.VMEM((rows, GLA_KW), F32),
            pltpu.VMEM((rows, GLA_VW), F32),
            pltpu.VMEM((rows, GLA_KW), F32),
            pltpu.VMEM((rows, GLA_VW), F32),
        ],
        compiler_params=_cparams(("arbitrary", "arbitrary")),
        name="gla",
    )(z, z, z, z, z, wa2_p, ba_p, on_p, jnp.asarray(e, BF16), jnp.asarray(e2, BF16), jnp.asarray(e.T),
      jnp.asarray(tr, BF16))


DSA_TQ = 256
DSA_VT_ROWS = LANES + 16
DSA_LOW_PASSES = 8


def _t5_bucket(dist):
    max_exact = NUM_BUCKETS // 2
    d = jnp.maximum(dist, 1).astype(F32)
    large = max_exact + (jnp.log(d / max_exact) / math.log(MAX_DISTANCE / max_exact)
                         * (NUM_BUCKETS - max_exact)).astype(I32)
    large = jnp.minimum(large, NUM_BUCKETS - 1)
    return jnp.where(dist < max_exact, dist, large)


def _block_loop(start, stop, body, init):
    npair = (stop - start) // 2
    carry = lax.fori_loop(0, npair, lambda i, c: body([start + 2 * i, start + 2 * i + 1], c), init)
    return lax.cond((stop - start) % 2 == 1, lambda c: body([stop - 1], c), lambda c: c, carry)


def _dsa_kernel(rb_ref, dq_ref, dk_ref, dv_ref, iq_ref, ik_ref, iw_ref, o_ref,
                bias_ref, key_ref, key16_ref, vt_ref, qp_ref, iqs_ref, lo_ref, hi_ref, clo_ref, chi_ref,
                m_ref, acc_ref, lga_ref, lgb_ref, *, seq, topk):
    TQ = DSA_TQ
    bi = pl.program_id(0)
    qb = pl.program_id(1)
    nkb = qb + 1
    log2_seq = int(math.log2(seq))

    s_row = lax.broadcasted_iota(I32, (TQ, TQ), 0)
    t_lane = lax.broadcasted_iota(I32, (TQ, TQ), 1)

    @pl.when((bi == 0) & (qb == 0))
    def _():
        for delta in range(2):
            bucket = _t5_bucket(jnp.maximum(delta * TQ + t_lane - s_row, 0))
            for hh in range(DSA_HEADS):
                tile = jnp.zeros((TQ, TQ), F32)
                for bk in range(NUM_BUCKETS):
                    tile = jnp.where(bucket == bk, rb_ref[bk, hh] * LOG2E, tile)
                bias_ref[hh, delta] = tile

    @pl.when(qb == 0)
    def _():
        ones_rows = (lax.broadcasted_iota(I32, (DSA_VT_ROWS - LANES, TQ), 0) == 0).astype(BF16)

        def prep(kb, carry):
            k0 = pl.multiple_of(kb * TQ, TQ)
            for pr in range(DSA_HEADS // 2):
                vt_ref[kb, pr, 0:LANES, :] = (
                    dv_ref[pl.ds(k0, TQ), pr * LANES:(pr + 1) * LANES].astype(F32).T.astype(BF16))
                vt_ref[kb, pr, LANES:DSA_VT_ROWS, :] = ones_rows
            return carry

        lax.fori_loop(0, seq // TQ, prep, 0)

    low_half = lax.broadcasted_iota(I32, (TQ, LANES), 1) < DSA_DH
    for hh in range(DSA_HEADS):
        pair = dq_ref[:, (hh // 2) * LANES:(hh // 2 + 1) * LANES]
        qp_ref[:, hh * LANES:(hh + 1) * LANES] = jnp.where(low_half == (hh % 2 == 0), pair,
                                                           jnp.zeros_like(pair))

    for hh in range(IDX_HEADS):
        iqs_ref[hh * TQ:(hh + 1) * TQ, :] = iq_ref[:, hh * IDX_DIM:(hh + 1) * IDX_DIM]
    w_t = iw_ref[...].astype(F32).T * ((IDX_HEADS ** -0.5) * (IDX_DIM ** -0.5))

    def score_blocks(kbs, carry):
        k0s = [pl.multiple_of(kb * TQ, TQ) for kb in kbs]
        dots = [_dot_nt(ik_ref[pl.ds(k0, TQ), :][:, :IDX_DIM], iqs_ref[...]) for k0 in k0s]
        for kb, k0, dt in zip(kbs, k0s, dots):
            sc = jnp.zeros((TQ, TQ), F32)
            for hh in range(IDX_HEADS):
                sc = sc + (jnp.maximum(dt[:, hh * TQ:(hh + 1) * TQ], 0.0)
                           * w_t[IDX_DIM + hh:IDX_DIM + hh + 1, :])
            bits = pltpu.bitcast(sc, I32)
            keys = bits ^ ((bits >> 31) & INT_MAX)
            causal = (k0 + s_row) <= (qb * TQ + t_lane)
            keys = jnp.where(causal, keys, INT_MIN)
            key_ref[kb] = keys
            key16_ref[kb] = (keys >> 16).astype(jnp.int16)
        return carry

    _block_loop(0, nkb, score_blocks, 0)

    ncausal = qb * TQ + lax.broadcasted_iota(I32, (1, TQ), 1) + 1
    short = ncausal <= topk

    def count_cols(pred, dtype, ref):
        rows = 8 * (4 // jnp.dtype(dtype).itemsize)

        def body(kbs, cnt):
            for kb in kbs:
                hit = jnp.where(pred(ref[kb], kb), jnp.ones((), dtype), jnp.zeros((), dtype))
                for r in range(TQ // rows):
                    cnt = cnt + hit[r * rows:(r + 1) * rows, :]
            return cnt

        cnt = _block_loop(0, nkb, body, jnp.zeros((rows, TQ), dtype))
        return jnp.sum(cnt.astype(I32), axis=0, keepdims=True)

    def bisect_step(state, count_ge):
        lo, hi, clo, chi = state
        mid = (lo >> 1) + (hi >> 1) + (lo & hi & 1)
        c = count_ge(mid)
        up = (c >= topk) & (mid != lo)
        down = c < topk
        return (jnp.where(up, mid, lo), jnp.where(down, mid, hi),
                jnp.where(up, c, clo), jnp.where(down, c, chi))

    def step16(_, state):
        return bisect_step(state, lambda mid: count_cols(
            lambda kk, kb: kk >= mid.astype(jnp.int16), jnp.int16, key16_ref))

    lo16, _, clo, chi = lax.fori_loop(
        0, 16, step16,
        (jnp.full((1, TQ), -2 ** 15 + 1, I32), jnp.full((1, TQ), 2 ** 15, I32),
         ncausal, jnp.zeros((1, TQ), I32)))
    lo16_h = lo16.astype(jnp.int16)

    def low_half(kbs, carry):
        for kb in kbs:
            low = ((key_ref[kb] & 0xFFFF) - 2 ** 15).astype(jnp.int16)
            key16_ref[kb] = jnp.where(key16_ref[kb] == lo16_h, low, jnp.int16(-2 ** 15))
        return carry

    _block_loop(0, nkb, low_half, 0)
    above = chi

    def step_low(state):
        frozen = state[2] == topk
        new = bisect_step(state, lambda mid: above + count_cols(
            lambda kk, kb: kk >= mid.astype(jnp.int16), jnp.int16, key16_ref))
        return tuple(jnp.where(frozen, o, n) for o, n in zip(state, new))

    st = lax.fori_loop(0, DSA_LOW_PASSES, lambda _, st_: step_low(st_),
                       (jnp.full((1, TQ), -2 ** 15, I32), jnp.full((1, TQ), 2 ** 15, I32), clo, chi))
    lo_ref[...], hi_ref[...], clo_ref[...], chi_ref[...] = st

    def active_count():
        open_q = ((clo_ref[...] != topk) & (hi_ref[...] - lo_ref[...] > 1)
                  & jnp.logical_not(short))
        return jnp.sum(open_q.astype(I32))

    def bis_body(state):
        it, _ = state
        st_ = (lo_ref[...], hi_ref[...], clo_ref[...], chi_ref[...])
        for _ in range(2):
            st_ = step_low(st_)
        lo_ref[...], hi_ref[...], clo_ref[...], chi_ref[...] = st_
        return it + 1, active_count()

    lax.while_loop(lambda s_: (s_[0] < (18 - DSA_LOW_PASSES) // 2) & (s_[1] > 0), bis_body,
                   (jnp.int32(0), active_count()))
    lo_ref[...] = jnp.where(short, INT_MIN + 1, (lo16 << 16) + (lo_ref[...] + 2 ** 15))

    tie_q = (clo_ref[...] > topk) & jnp.logical_not(short)

    @pl.when(jnp.sum(tie_q.astype(I32)) > 0)
    def _():
        thr = lo_ref[...]
        need = topk - chi_ref[...]
        plo = jnp.full((1, TQ), -1, I32)
        phi = jnp.full((1, TQ), seq - 1, I32)
        for _ in range(log2_seq):
            pm = (plo + phi) >> 1
            c = count_cols(lambda kk, kb: (kk == thr) & ((kb * TQ + s_row) <= pm), I32, key_ref)
            ok = c >= need
            phi = jnp.where(ok, pm, phi)
            plo = jnp.where(ok, plo, pm)
        cut = jnp.where(tie_q, phi, seq - 1)

        def demote(kb, carry):
            kk = key_ref[kb]
            drop = (kk == thr) & ((kb * TQ + s_row) > cut)
            key_ref[kb] = jnp.where(drop, INT_MIN, kk)
            return carry

        lax.fori_loop(0, nkb, demote, 0)

    m_ref[...] = jnp.full(m_ref.shape, NEG, F32)
    acc_ref[...] = jnp.zeros_like(acc_ref)
    thr_sel = lo_ref[...]

    heads = range(DSA_HEADS)

    def qk(kb, hh):
        k0 = pl.multiple_of(kb * TQ, TQ)
        return _dot_nt(dk_ref[pl.ds(k0, TQ), (hh // 2) * LANES:(hh // 2 + 1) * LANES],
                       qp_ref[:, hh * LANES:(hh + 1) * LANES])

    def softmax_update(kb, logits, shift):
        sel = key_ref[kb] >= thr_sel
        m_prev = [m_ref[hh] for hh in heads]
        acc_prev = [acc_ref[hh] for hh in heads]
        lg = [jnp.where(sel, logits[hh], NEG) for hh in heads]
        m_new = [jnp.maximum(m_prev[hh], jnp.max(lg[hh], axis=0, keepdims=True) + shift[hh])
                 for hh in heads]
        p = [jnp.exp2(lg[hh] - (m_new[hh] - shift[hh])).astype(BF16) for hh in heads]
        pv = [_dot(vt_ref[kb, hh // 2], p[hh]) for hh in heads]
        for hh in heads:
            acc_ref[hh] = acc_prev[hh] * jnp.exp2(m_prev[hh] - m_new[hh]) + pv[hh]
            m_ref[hh] = m_new[hh]

    n_far = jnp.maximum(qb - 1, 0)
    far_shift = [rb_ref[NUM_BUCKETS - 1, hh] * LOG2E for hh in heads]

    @pl.when(n_far > 0)
    def _():
        for hh in heads:
            lga_ref[hh] = qk(0, hh)

    def far_pair(i, carry):
        ka = 2 * i
        for hh in heads:
            lgb_ref[hh] = qk(ka + 1, hh)
        softmax_update(ka, [lga_ref[hh] for hh in heads], far_shift)
        for hh in heads:
            lga_ref[hh] = qk(jnp.minimum(ka + 2, n_far - 1), hh)
        softmax_update(ka + 1, [lgb_ref[hh] for hh in heads], far_shift)
        return carry

    lax.fori_loop(0, n_far // 2, far_pair, 0)

    @pl.when(n_far % 2 == 1)
    def _():
        softmax_update(n_far - 1, [lga_ref[hh] for hh in heads], far_shift)

    def near_body(kb, carry):
        softmax_update(kb, [qk(kb, hh) + bias_ref[hh, qb - kb] for hh in heads], [0.0] * DSA_HEADS)
        return carry

    lax.fori_loop(n_far, nkb, near_body, 0)

    outs = []
    for hh in range(DSA_HEADS):
        a = acc_ref[hh]
        r0 = (hh % 2) * DSA_DH
        outs.append(a[r0:r0 + DSA_DH, :] / a[LANES:LANES + 1, :])
    o_ref[...] = jnp.concatenate(outs, axis=0).T.astype(BF16)


def _dsa(z, rel_bias, B, S):
    T = z.shape[0]
    TQ = DSA_TQ
    nqb = S // TQ
    topk = min(TOPK_MAX, S // 4)
    qrow = lambda b, q: b * nqb + q
    return pl.pallas_call(
        functools.partial(_dsa_kernel, seq=S, topk=topk),
        out_shape=jax.ShapeDtypeStruct((T, DSA_HEADS * DSA_DH), BF16),
        grid=(B, nqb),
        in_specs=[
            pl.BlockSpec(memory_space=pltpu.SMEM),
            pl.BlockSpec((TQ, DSA_W), lambda b, q: (qrow(b, q), OFF_DQ // DSA_W)),
            pl.BlockSpec((S, DSA_W), lambda b, q: (b, OFF_DK // DSA_W)),
            pl.BlockSpec((S, DSA_W), lambda b, q: (b, OFF_DV // DSA_W)),
            pl.BlockSpec((TQ, IQ_W), lambda b, q: (qrow(b, q), OFF_IQ // IQ_W)),
            pl.BlockSpec((S, LANES), lambda b, q: (b, OFF_IK // LANES)),
            pl.BlockSpec((TQ, LANES), lambda b, q: (qrow(b, q), OFF_IK // LANES)),
        ],
        out_specs=pl.BlockSpec((TQ, DSA_HEADS * DSA_DH), lambda b, q: (qrow(b, q), 0)),
        scratch_shapes=[
            pltpu.VMEM((DSA_HEADS, 2, TQ, TQ), F32),
            pltpu.VMEM((S // TQ, TQ, TQ), I32),
            pltpu.VMEM((S // TQ, TQ, TQ), jnp.int16),
            pltpu.VMEM((S // TQ, DSA_HEADS // 2, DSA_VT_ROWS, TQ), BF16),
            pltpu.VMEM((TQ, DSA_WP), BF16),
            pltpu.VMEM((IDX_HEADS * TQ, IDX_DIM), BF16),
            pltpu.VMEM((1, TQ), I32),
            pltpu.VMEM((1, TQ), I32),
            pltpu.VMEM((1, TQ), I32),
            pltpu.VMEM((1, TQ), I32),
            pltpu.VMEM((DSA_HEADS, 1, TQ), F32),
            pltpu.VMEM((DSA_HEADS, DSA_VT_ROWS, TQ), F32),
            pltpu.VMEM((DSA_HEADS, TQ, TQ), F32),
            pltpu.VMEM((DSA_HEADS, TQ, TQ), F32),
        ],
        compiler_params=_cparams(("arbitrary", "arbitrary")),
        name="dsa",
    )(rel_bias, z, z, z, z, z, z)


CONV_HALO = 32


def _conv_kernel(u_ref, w_ref, cb_ref, g_ref, b_ref, o_ref, h_ref, *, tm):
    @pl.when(pl.program_id(1) == 0)
    def _():
        h_ref[0:CONV_HALO, :] = jnp.zeros((CONV_HALO, CONV_CH), F32)

    u = u_ref[...].astype(F32)
    a = u[:, :CONV_CH]
    g = u[:, CONV_CH:]
    h_ref[CONV_HALO:CONV_HALO + tm, :] = a / (1.0 + jnp.exp(-g))
    acc = jnp.zeros((tm, CONV_CH), F32) + cb_ref[...]
    base = CONV_HALO - (CONV_WIDTH - 1)
    for j in range(CONV_WIDTH):
        acc = acc + h_ref[base + j:base + j + tm, :] * w_ref[j:j + 1, :]
    h_ref[0:CONV_HALO, :] = h_ref[tm:tm + CONV_HALO, :]
    mu = jnp.mean(acc, axis=-1, keepdims=True)
    d = acc - mu
    var = jnp.mean(d * d, axis=-1, keepdims=True)
    y = d * lax.rsqrt(var + NORM_EPS) * g_ref[...] + b_ref[...]
    o_ref[...] = (y / (1.0 + jnp.exp(-y))).astype(BF16)


def _conv(z, w, cb, g, b, B, S, tm):
    T = z.shape[0]
    nblk = S // tm
    const = lambda bb, c: (0, 0)
    return pl.pallas_call(
        functools.partial(_conv_kernel, tm=tm),
        out_shape=jax.ShapeDtypeStruct((T, CONV_CH), BF16),
        grid=(B, nblk),
        in_specs=[
            pl.BlockSpec((tm, CU_W), lambda bb, c: (bb * nblk + c, OFF_CU // CU_W)),
            pl.BlockSpec((CONV_WIDTH, CONV_CH), const),
            pl.BlockSpec((1, CONV_CH), const),
            pl.BlockSpec((1, CONV_CH), const),
            pl.BlockSpec((1, CONV_CH), const),
        ],
        out_specs=pl.BlockSpec((tm, CONV_CH), lambda bb, c: (bb * nblk + c, 0)),
        scratch_shapes=[pltpu.VMEM((tm + CONV_HALO, CONV_CH), F32)],
        compiler_params=_cparams(("arbitrary", "arbitrary")),
        name="conv",
    )(z, w, cb, g, b)


XW = X_HEADS * X_DH


def _mem_kv_kernel(m_ref, g_ref, w_ref, kn_ref, e_ref, k_ref, v_ref, *, mlen):
    hm = _rms(m_ref[...], g_ref[...]).astype(BF16)
    kv = _dot(hm, w_ref[...])
    k = kv[:, :XW]
    v = kv[:, XW:]
    ss = _dot_exact_rhs(k * k, e_ref[...], 2) * (1.0 / X_DH)
    k = k * lax.rsqrt(ss + NORM_EPS) * kn_ref[...] * (X_DH ** -0.5)
    lane_head = lax.broadcasted_iota(I32, (mlen, XW), 1) // X_DH
    for hh in range(X_HEADS):
        msk = lane_head == hh
        k_ref[hh * mlen:(hh + 1) * mlen, :] = jnp.where(msk, k, 0.0).astype(BF16)
        v_ref[hh * mlen:(hh + 1) * mlen, :] = jnp.where(msk, v, 0.0).astype(BF16)


def _mem_kv(mem2d, g, w_ckv, kn_t, e_x, B, M):
    D = mem2d.shape[1]
    const = lambda b: (0, 0)
    out = jax.ShapeDtypeStruct((B * X_HEADS * M, XW), BF16)
    return pl.pallas_call(
        functools.partial(_mem_kv_kernel, mlen=M),
        out_shape=(out, out),
        grid=(B,),
        in_specs=[
            pl.BlockSpec((M, D), lambda b: (b, 0)),
            pl.BlockSpec((1, D), const),
            pl.BlockSpec((D, 2 * XW), const),
            pl.BlockSpec((1, XW), const),
            pl.BlockSpec((XW, XW), const),
        ],
        out_specs=(pl.BlockSpec((X_HEADS * M, XW), lambda b: (b, 0)),
                   pl.BlockSpec((X_HEADS * M, XW), lambda b: (b, 0))),
        compiler_params=_cparams(("arbitrary",)),
        name="mem_kv",
    )(mem2d, g, w_ckv, kn_t, e_x)


def _post_kernel(x_ref, og_ref, od_ref, oc_ref, wo_ref, nx_ref, wcq_ref, qn_ref, e_ref,
                 kbd_ref, vbd_ref, obd_ref, wco_ref, nf_ref, wr_ref, br_ref,
                 x2_ref, hf_ref, route_ref, *, mlen):
    wo = wo_ref
    x1 = (x_ref[...]
          + _dot(og_ref[...], wo[0:GLA_VW, :])
          + _dot(od_ref[...], wo[GLA_VW:2 * GLA_VW, :])
          + _dot(oc_ref[...], wo[2 * GLA_VW:, :]))
    hx = _rms(x1, nx_ref[...]).astype(BF16)
    q = _dot(hx, wcq_ref[...])
    ss = _dot_exact_rhs(q * q, e_ref[...], 2) * (1.0 / X_DH)
    q = (q * lax.rsqrt(ss + NORM_EPS) * qn_ref[...]).astype(BF16)
    logits = _dot_nt(q, kbd_ref[...])
    ps = []
    for hh in range(X_HEADS):
        lg = logits[:, hh * mlen:(hh + 1) * mlen]
        ps.append(jnp.exp(lg - jnp.max(lg, axis=-1, keepdims=True)))
    p = jnp.concatenate(ps, axis=-1).astype(BF16)
    o = _dot(p, vbd_ref[...]) / _dot(p, obd_ref[...])
    x2 = x1 + _dot(o.astype(BF16), wco_ref[...])
    x2_ref[...] = x2
    hf = _rms(x2, nf_ref[...])
    rl = lax.dot_general(wr_ref[...], hf, (((1,), (1,)), ((), ())), preferred_element_type=F32,
                         precision=lax.Precision.HIGHEST) + br_ref[...]
    row = lambda r: rl[r:r + 1, :]
    gl = [row(N_EXPERTS + g) for g in range(N_GROUPS)]
    g_max = functools.reduce(jnp.maximum, gl)
    g_sel = _first_index(gl, g_max)
    g_w = 1.0 / functools.reduce(lambda a, b: a + b, [jnp.exp(g - g_max) for g in gl])
    el = [_select_by_index(g_sel, [row(g * EXPERTS_PER_GROUP + j) for g in range(N_GROUPS)])
          for j in range(EXPERTS_PER_GROUP)]
    e1 = functools.reduce(jnp.maximum, el)
    i1 = _first_index(el, e1)
    rest = [jnp.where(i1 == j, NEG, el[j]) for j in range(EXPERTS_PER_GROUP)]
    e2 = functools.reduce(jnp.maximum, rest)
    i2 = _first_index(rest, e2)
    p2 = jnp.exp(e2 - e1)
    w1 = g_w / (1.0 + p2)
    w2 = g_w * p2 / (1.0 + p2)
    a = jnp.minimum(i1, i2)
    b = jnp.maximum(i1, i2)
    pair = jnp.where(a == 0, 0, jnp.where(a == 1, 3, 5)) + b - a - 1
    bucket = g_sel * MOE_PAIRS + pair
    rows = jnp.concatenate([bucket.astype(F32), jnp.where(i1 < i2, w1, w2), jnp.where(i1 < i2, w2, w1)],
                           axis=0)
    route_ref[...] = jnp.concatenate([rows, jnp.zeros((8 - MOE_ROUTE_ROWS, rows.shape[1]), F32)], axis=0)
    cols = jnp.concatenate([rows, jnp.zeros((LANES - MOE_ROUTE_ROWS, rows.shape[1]), F32)], axis=0).T
    hf_ref[...] = jnp.concatenate([hf, cols], axis=1)


def _first_index(rows, value):
    idx = jnp.full(value.shape, len(rows) - 1, I32)
    for j in range(len(rows) - 2, -1, -1):
        idx = jnp.where(rows[j] == value, j, idx)
    return idx


def _select_by_index(idx, rows):
    out = rows[-1]
    for j in range(len(rows) - 2, -1, -1):
        out = jnp.where(idx == j, rows[j], out)
    return out


def _post(x2d, o_gla, o_dsa, o_conv, w_out, nx, w_cq, qn_t, e_x, kbd, vbd, obd, w_co, nf, w_r, b_r,
          B, S, M, tm):
    T, D = x2d.shape
    nblk = S // tm
    const = lambda i: (0, 0)
    row = lambda i: (i, 0)
    bat = lambda i: (i // nblk, 0)
    HM = X_HEADS * M
    return pl.pallas_call(
        functools.partial(_post_kernel, mlen=M),
        out_shape=(jax.ShapeDtypeStruct((T, D), F32), jax.ShapeDtypeStruct((T, D + LANES), F32),
                   jax.ShapeDtypeStruct((8, T), F32)),
        grid=(T // tm,),
        in_specs=[
            pl.BlockSpec((tm, D), row),
            pl.BlockSpec((tm, GLA_VW), row),
            pl.BlockSpec((tm, DSA_HEADS * DSA_DH), row),
            pl.BlockSpec((tm, CONV_CH), row),
            pl.BlockSpec((D, D), const),
            pl.BlockSpec((1, D), const),
            pl.BlockSpec((D, XW), const),
            pl.BlockSpec((1, XW), const),
            pl.BlockSpec((XW, XW), const),
            pl.BlockSpec((HM, XW), bat),
            pl.BlockSpec((HM, XW), bat),
            pl.BlockSpec((HM, XW), const),
            pl.BlockSpec((XW, D), const),
            pl.BlockSpec((1, D), const),
            pl.BlockSpec((LANES, D), const),
            pl.BlockSpec((LANES, 1), const),
        ],
        out_specs=(pl.BlockSpec((tm, D), row), pl.BlockSpec((tm, D + LANES), row),
                   pl.BlockSpec((8, tm), lambda i: (0, i))),
        compiler_params=_cparams(("arbitrary",)),
        name="post",
    )(x2d, o_gla, o_dsa, o_conv, w_out, nx, w_cq, qn_t, e_x, kbd, vbd, obd, w_co, nf, w_r, b_r)


MOE_TR = 256
MOE_RANK_TL = 512
BUCKET_ROWS = 32


def _moe_rank_kernel(route_ref, triu_ref, rank_ref, count_ref, carry_ref):
    @pl.when(pl.program_id(0) == 0)
    def _():
        carry_ref[...] = jnp.zeros_like(carry_ref)

    bucket = route_ref[0:1, :].astype(I32)
    onehot = (lax.broadcasted_iota(I32, (BUCKET_ROWS, MOE_RANK_TL), 0) == bucket)
    incl = _dot(onehot.astype(BF16), triu_ref[...])
    carry = carry_ref[...]
    before = jnp.where(onehot, incl - 1.0 + carry[:, 0:1], 0.0)
    rank_ref[...] = jnp.sum(before, axis=0, keepdims=True).astype(I32)
    carry = carry + incl[:, MOE_RANK_TL - 1:MOE_RANK_TL]
    carry_ref[...] = carry
    count_ref[...] = carry


def _moe_rank(route):
    T = route.shape[1]
    tl = MOE_RANK_TL
    triu = jnp.asarray(np.triu(np.ones((tl, tl), np.float32)), BF16)
    return pl.pallas_call(
        _moe_rank_kernel,
        out_shape=(jax.ShapeDtypeStruct((1, T), I32), jax.ShapeDtypeStruct((BUCKET_ROWS, LANES), F32)),
        grid=(T // tl,),
        in_specs=[pl.BlockSpec((8, tl), lambda i: (0, i)), pl.BlockSpec((tl, tl), lambda i: (0, 0))],
        out_specs=(pl.BlockSpec((1, tl), lambda i: (0, i)),
                   pl.BlockSpec((BUCKET_ROWS, LANES), lambda i: (0, 0))),
        scratch_shapes=[pltpu.VMEM((BUCKET_ROWS, LANES), F32)],
        compiler_params=_cparams(("arbitrary",)),
        name="moe_rank",
    )(route, triu)


def _moe_scatter_kernel(slot_ref, h_ref, xs0_ref, xs_ref, sem, *, tm):
    del xs0_ref
    base = pl.program_id(0) * tm

    def body(r, carry):
        pltpu.make_async_copy(h_ref.at[pl.ds(r, 1)], xs_ref.at[pl.ds(slot_ref[base + r], 1)], sem).start()
        return carry

    lax.fori_loop(0, tm, body, 0, unroll=8)
    pltpu.make_async_copy(h_ref, xs_ref.at[pl.ds(0, tm)], sem).wait()


def _moe_scatter(slot, h, nrows, tm):
    T, W = h.shape
    return pl.pallas_call(
        functools.partial(_moe_scatter_kernel, tm=tm),
        out_shape=jax.ShapeDtypeStruct((nrows, W), h.dtype),
        grid_spec=pltpu.PrefetchScalarGridSpec(
            num_scalar_prefetch=1, grid=(T // tm,),
            in_specs=[pl.BlockSpec((tm, W), lambda i, sl: (i, 0)),
                      pl.BlockSpec(memory_space=pl.ANY)],
            out_specs=pl.BlockSpec(memory_space=pl.ANY),
            scratch_shapes=[pltpu.SemaphoreType.DMA(())]),
        input_output_aliases={2: 0},
        compiler_params=_cparams(("arbitrary",)),
        name="moe_scatter",
    )(slot, h, jnp.zeros((nrows, W), h.dtype))


def _moe_experts_kernel(elo_ref, ehi_ref, used_ref, xs_ref, wgl_ref, wul_ref, wdl_ref,
                        wgh_ref, wuh_ref, wdh_ref, o_ref, *, d_model):
    used = used_ref[pl.program_id(0)] > 0

    @pl.when(jnp.logical_not(used))
    def _():
        o_ref[...] = jnp.zeros_like(o_ref)

    @pl.when(used)
    def _():
        t = xs_ref[:, :d_model].astype(BF16)
        y = None
        for col, (wg, wu, wd) in enumerate(((wgl_ref, wul_ref, wdl_ref), (wgh_ref, wuh_ref, wdh_ref))):
            g = xs_ref[:, d_model + 1 + col:d_model + 2 + col]
            a = _dot(t, wg[0])
            u = _dot(t, wu[0])
            he = (a / (1.0 + jnp.exp(-a))) * u * g
            part = _dot(he.astype(BF16), wd[0])
            y = part if y is None else y + part
        o_ref[...] = y


def _moe_experts(e_lo, e_hi, used, xs, wg, wu, wd):
    nrows, W = xs.shape
    D, DE = wg.shape[-2:]
    tr = MOE_TR
    lo = lambda j, el, eh, us: (el[j], 0, 0)
    hi = lambda j, el, eh, us: (eh[j], 0, 0)
    return pl.pallas_call(
        functools.partial(_moe_experts_kernel, d_model=D),
        out_shape=jax.ShapeDtypeStruct((nrows, D), F32),
        grid_spec=pltpu.PrefetchScalarGridSpec(
            num_scalar_prefetch=3, grid=(nrows // tr,),
            in_specs=[
                pl.BlockSpec((tr, W), lambda j, el, eh, us: (j, 0)),
                pl.BlockSpec((1, D, DE), lo), pl.BlockSpec((1, D, DE), lo), pl.BlockSpec((1, DE, D), lo),
                pl.BlockSpec((1, D, DE), hi), pl.BlockSpec((1, D, DE), hi), pl.BlockSpec((1, DE, D), hi),
            ],
            out_specs=pl.BlockSpec((tr, D), lambda j, el, eh, us: (j, 0))),
        compiler_params=_cparams(("arbitrary",)),
        name="moe_experts",
    )(e_lo, e_hi, used, xs, wg, wu, wd, wg, wu, wd)


def _row_gather(idx_ref, first, src_ref, buf_ref, sem, rows, *, start):
    if not start:
        pltpu.make_async_copy(src_ref.at[pl.ds(0, rows)], buf_ref, sem).wait()
        return

    def body(r, carry):
        pltpu.make_async_copy(src_ref.at[pl.ds(idx_ref[first + r], 1)], buf_ref.at[pl.ds(r, 1)],
                              sem).start()
        return carry

    lax.fori_loop(0, rows, body, 0, unroll=8)


def _moe_combine_kernel(slot_ref, x2_ref, ys_ref, o_ref, buf_ref, sem, *, tm, ntile):
    i = pl.program_id(0)

    def fetch(tile, slot, start):
        _row_gather(slot_ref, tile * tm, ys_ref, buf_ref.at[slot], sem.at[slot], tm, start=start)

    @pl.when(i == 0)
    def _():
        fetch(0, 0, True)

    @pl.when(i + 1 < ntile)
    def _():
        fetch(i + 1, (i + 1) % 2, True)

    fetch(i, i % 2, False)
    o_ref[...] = x2_ref[...] + buf_ref[i % 2]


def _moe_combine(slot, x2, ys, tm):
    T, D = x2.shape
    ntile = T // tm
    return pl.pallas_call(
        functools.partial(_moe_combine_kernel, tm=tm, ntile=ntile),
        out_shape=jax.ShapeDtypeStruct((T, D), F32),
        grid_spec=pltpu.PrefetchScalarGridSpec(
            num_scalar_prefetch=1, grid=(ntile,),
            in_specs=[pl.BlockSpec((tm, D), lambda i, sl: (i, 0)),
                      pl.BlockSpec(memory_space=pl.ANY)],
            out_specs=pl.BlockSpec((tm, D), lambda i, sl: (i, 0)),
            scratch_shapes=[pltpu.VMEM((2, tm, D), F32), pltpu.SemaphoreType.DMA((2,))]),
        compiler_params=_cparams(("arbitrary",)),
        name="moe_combine",
    )(slot, x2, ys)


def _moe(x2, hf_ext, route, wg, wu, wd):
    T, D = x2.shape
    tr = MOE_TR
    ntile = T // tr + MOE_BUCKETS
    bucket = route[0].astype(I32)
    rank, counts = _moe_rank(route)
    counts = counts[:MOE_BUCKETS, 0].astype(I32)
    ntl = (counts + tr - 1) // tr
    tile_end = jnp.cumsum(ntl)
    start = (tile_end - ntl) * tr
    onehot = bucket[:, None] == jnp.arange(MOE_BUCKETS, dtype=I32)[None, :]
    slot = jnp.sum(jnp.where(onehot, start[None, :], 0), axis=1) + rank[0]
    tile_id = jnp.arange(ntile, dtype=I32)
    tile_bucket = jnp.minimum(jnp.sum(tile_id[:, None] >= tile_end[None, :], axis=1),
                              MOE_BUCKETS - 1).astype(I32)
    used = (tile_id < tile_end[-1]).astype(I32)
    pair = tile_bucket % MOE_PAIRS
    pair_lo = (pair >= 3).astype(I32) + (pair >= 5).astype(I32)
    pair_hi = pair - jnp.where(pair_lo == 0, 0, jnp.where(pair_lo == 1, 3, 5)) + pair_lo + 1
    e_lo = (tile_bucket // MOE_PAIRS) * EXPERTS_PER_GROUP + pair_lo
    e_hi = (tile_bucket // MOE_PAIRS) * EXPERTS_PER_GROUP + pair_hi
    xs = _moe_scatter(slot, hf_ext, ntile * tr, tm=tr)
    ys = _moe_experts(e_lo, e_hi, used, xs, wg, wu, wd)
    return _moe_combine(slot, x2, ys, tm=tr)


def _pad_heads(w, heads, dh, slot):
    lead = w.shape[:-1]
    w = w.reshape(lead + (heads, dh))
    w = jnp.pad(w, [(0, 0)] * len(lead) + [(0, 0), (0, slot - dh)])
    return w.reshape(lead + (heads * slot,))


def _pad_cols(w, width):
    return jnp.pad(w, [(0, 0)] * (w.ndim - 1) + [(0, width - w.shape[-1])])


def _layout_w_in(w):
    sizes = (GLA_HEADS * GLA_DK, GLA_HEADS * GLA_DK, GLA_VW, GLA_RANK, GLA_VW,
             DSA_HEADS * DSA_DH, DSA_HEADS * DSA_DH, DSA_HEADS * DSA_DH, IQ_W, IDX_DIM, IDX_HEADS,
             CU_W)
    offs = np.cumsum((0,) + sizes)
    g_q, g_k, g_v, g_lr, g_r, d_q, d_k, d_v, i_q, i_k, i_w, c_u = [
        w[:, offs[n]:offs[n + 1]] for n in range(len(sizes))]
    cols = [
        d_q, d_k, d_v, g_v, _pad_cols(g_r, GLA_VW + LANES),
        _pad_heads(g_q, GLA_HEADS, GLA_DK, GLA_DKP),
        _pad_heads(g_k, GLA_HEADS, GLA_DK, GLA_DKP),
        i_q, c_u,
        _pad_cols(g_lr, LANES),
        _pad_cols(jnp.concatenate([i_k, i_w], axis=-1), LANES),
    ]
    out = jnp.concatenate(cols, axis=-1)
    assert out.shape[-1] == NP_IN
    return out.astype(BF16)


def _block_ones(n, blk):
    idx = np.arange(n) // blk
    return jnp.asarray((idx[:, None] == idx[None, :]).astype(np.float32))


def _layer(x2d, mem2d, rel_bias, p, B, S, M):
    D = x2d.shape[1]
    row = lambda v: v.reshape(1, -1)
    qk_gain = jnp.concatenate([jnp.tile(p["dsa_qnorm"], DSA_HEADS) * (DSA_DH ** -0.5 * LOG2E),
                               jnp.tile(p["dsa_knorm"], DSA_HEADS)])
    z = _in_proj(x2d, row(p["norm_mix"]), _layout_w_in(p["w_in"]), row(qk_gain), tm=min(512, S))
    wa2_p = jnp.pad(_pad_heads(p["gla_wa2"], GLA_HEADS, GLA_DK, GLA_DKP),
                    ((0, LANES - GLA_RANK), (0, 0)))
    wa2_hi = wa2_p.astype(BF16)
    wa2_p = jnp.stack([wa2_hi, (wa2_p - wa2_hi.astype(F32)).astype(BF16)])
    o_gla = _gla(z, wa2_p, row(_pad_heads(p["gla_ba"], GLA_HEADS, GLA_DK, GLA_DKP)),
                 row(jnp.tile(p["gla_onorm"], GLA_HEADS)), B, S, rows=min(256, S))
    o_dsa = _dsa(z, rel_bias, B, S)
    o_conv = _conv(z, p["conv_w"], row(p["conv_b"]), row(p["conv_ln_g"]), row(p["conv_ln_b"]),
                   B, S, tm=min(512, S))
    e_x = _block_ones(XW, X_DH).astype(BF16)
    kbd, vbd = _mem_kv(mem2d, row(p["norm_mem"]), p["w_ckv"].astype(BF16),
                       row(jnp.tile(p["x_knorm"], X_HEADS)), e_x, B, M)
    obd = (jnp.arange(X_HEADS * M)[:, None] // M == jnp.arange(XW)[None, :] // X_DH).astype(BF16)
    w_r = _pad_cols(jnp.concatenate([p["w_re"], p["w_rg"]], axis=-1), LANES).T
    b_r = _pad_cols(row(jnp.concatenate([p["b_re"], p["b_rg"]])), LANES).T
    x2, hf, route = _post(
        x2d, o_gla, o_dsa, o_conv, p["w_out"].astype(BF16), row(p["norm_x"]),
        p["w_cq"].astype(BF16), row(jnp.tile(p["x_qnorm"], X_HEADS)), e_x, kbd, vbd, obd,
        p["w_co"].astype(BF16), row(p["norm_ffn"]), w_r, b_r, B, S, M, tm=min(512, S))
    return _moe(x2, hf, route, p["w_gate"].astype(BF16), p["w_up"].astype(BF16),
                p["w_down"].astype(BF16))


_LAYER_KEYS = ("norm_mix", "w_in", "gla_wa2", "gla_ba", "gla_onorm", "dsa_qnorm", "dsa_knorm",
               "conv_w", "conv_b", "conv_ln_g", "conv_ln_b", "w_out", "norm_x", "norm_mem", "w_cq",
               "w_ckv", "x_qnorm", "x_knorm", "w_co", "norm_ffn", "w_rg", "b_rg", "w_re", "b_re",
               "w_gate", "w_up", "w_down")


def kernel(x, mem, rel_bias, norm_mix, w_in, gla_wa2, gla_ba, gla_onorm, dsa_qnorm, dsa_knorm,
           conv_w, conv_b, conv_ln_g, conv_ln_b, w_out, norm_x, norm_mem, w_cq, w_ckv, x_qnorm,
           x_knorm, w_co, norm_ffn, w_rg, b_rg, w_re, b_re, w_gate, w_up, w_down):
    stacked = dict(zip(_LAYER_KEYS, (
        norm_mix, w_in, gla_wa2, gla_ba, gla_onorm, dsa_qnorm, dsa_knorm, conv_w, conv_b,
        conv_ln_g, conv_ln_b, w_out, norm_x, norm_mem, w_cq, w_ckv, x_qnorm, x_knorm, w_co,
        norm_ffn, w_rg, b_rg, w_re, b_re, w_gate, w_up, w_down)))
    B, S, D = x.shape
    M = mem.shape[1]
    x2d = x.reshape(B * S, D)
    mem2d = mem.reshape(B * M, D)
    for l in range(norm_mix.shape[0]):
        x2d = _layer(x2d, mem2d, rel_bias, {k: v[l] for k, v in stacked.items()}, B, S, M)
    return x2d.reshape(B, S, D)
```
